```python
import math
import jax, jax.numpy as jnp
from jax import lax
import numpy as np

D_MODEL = 4096
BATCH = 4
SEQ = 4096
DEPTH = 1

GRID_W = 64
MIX_WIDTH = D_MODEL
HG_WIDTH = MIX_WIDTH // 2
NA_WIDTH = MIX_WIDTH - HG_WIDTH
HG_HEAD_DIM = 128
HG_HEADS = HG_WIDTH // HG_HEAD_DIM
NA_HEAD_DIM = 128
NA_HEADS = NA_WIDTH // NA_HEAD_DIM
NA_WIN_ROWS = 8
NA_WIN_COLS = 16
HG_CHUNK = 64
D_FF = ((8 * D_MODEL // 3 + 255) // 256) * 256
FFN_RES_WEIGHT = 0.5
EPS = 1e-6
IN_COLS = 5 * HG_WIDTH + 3 * NA_WIDTH
IN_SPLITS = [HG_WIDTH, 2 * HG_WIDTH, 3 * HG_WIDTH, 4 * HG_WIDTH, 5 * HG_WIDTH,
             5 * HG_WIDTH + NA_WIDTH, 5 * HG_WIDTH + 2 * NA_WIDTH]

kernel_name = "hybrid_hgrn2_natten_macaron_block"


def rmsnorm(x, gain):
    xf = x.astype(jnp.float32)
    y = xf * lax.rsqrt(jnp.mean(xf * xf, axis=-1, keepdims=True) + EPS)
    return (y * gain.astype(jnp.float32)).astype(x.dtype)


def swiglu(x, w_gate, w_up, w_down):
    return (jax.nn.silu(x @ w_gate) * (x @ w_up)) @ w_down


def gla_chunkwise(q, k, v, log_f):
    B, T, H, DK = q.shape
    DV = v.shape[-1]
    n_chunks = T // HG_CHUNK

    def to_chunks(a):
        return a.reshape(B, n_chunks, HG_CHUNK, H, a.shape[-1]).transpose(1, 0, 3, 2, 4)

    xs = tuple(to_chunks(a) for a in (q, k, v, log_f))
    tri = jnp.tril(jnp.ones((HG_CHUNK, HG_CHUNK), dtype=bool))

    def step(S, chunk):
        qb, kb, vb, gb = chunk
        b = jnp.cumsum(gb, axis=-2)
        diff = b[..., :, None, :] - b[..., None, :, :]
        decay = jnp.exp(jnp.where(tri[:, :, None], diff, -jnp.inf))
        scores = jnp.einsum('bhtk,bhsk,bhtsk->bhts', qb, kb, decay)
        o = (jnp.einsum('bhts,bhsv->bhtv', scores, vb)
             + jnp.einsum('bhtk,bhkv->bhtv', qb * jnp.exp(b), S))
        b_last = b[..., -1:, :]
        S = (jnp.exp(b_last[..., 0, :])[..., None] * S
             + jnp.einsum('bhsk,bhsv->bhkv', kb * jnp.exp(b_last - b), vb))
        return S, o

    S0 = jnp.zeros((B, H, DK, DV), jnp.float32)
    _, o = lax.scan(step, S0, xs)
    return o.transpose(1, 0, 3, 2, 4).reshape(B, T, H, DV)


def hgrn2_bidirectional(q_raw, f_fwd_raw, f_bwd_raw, i_raw, g_raw, lb_logits, layer, head_norm):
    B, T, _ = q_raw.shape
    out_dtype = q_raw.dtype
    lb = jnp.cumsum(jax.nn.softmax(lb_logits.astype(jnp.float32), axis=0), axis=0)[layer]

    def heads(a):
        return a.astype(jnp.float32).reshape(B, T, HG_HEADS, HG_HEAD_DIM)

    qh = heads(jax.nn.silu(q_raw))
    vh = heads(i_raw)

    def direction(f_raw, lb_d, reverse):
        f = lb_d + (1.0 - lb_d) * jax.nn.sigmoid(f_raw.astype(jnp.float32))
        fh = heads(f)
        args = (qh, 1.0 - fh, vh, jnp.log(fh))
        if reverse:
            args = tuple(jnp.flip(a, axis=1) for a in args)
        o = gla_chunkwise(*args)
        return jnp.flip(o, axis=1) if reverse else o

    o = direction(f_fwd_raw, lb[0], False) + direction(f_bwd_raw, lb[1], True)
    o = rmsnorm(o, head_norm) * jax.nn.silu(heads(g_raw))
    return o.reshape(B, T, HG_WIDTH).astype(out_dtype)


def neighbourhood_attention_2d(q, k, v, rpb):
    B, T, _ = q.shape
    rows = T // GRID_W
    kr = min(NA_WIN_ROWS, rows)
    kc = NA_WIN_COLS

    def grid(a):
        return a.reshape(B, rows, GRID_W, NA_HEADS, NA_HEAD_DIM).transpose(0, 3, 1, 2, 4)

    qg, kg, vg = grid(q), grid(k), grid(v)
    col = jnp.arange(GRID_W)
    col_start = jnp.clip(col - kc // 2, 0, GRID_W - kc)
    col_mask = (col[None, :] >= col_start[:, None]) & (col[None, :] < col_start[:, None] + kc)
    dc_idx = jnp.clip(col[None, :] - col[:, None] + NA_WIN_COLS - 1, 0, 2 * NA_WIN_COLS - 2)
    rpb32 = rpb.astype(jnp.float32)
    scale = NA_HEAD_DIM ** -0.5

    def one_row(args):
        r, q_row = args
        row_start = jnp.clip(r - kr // 2, 0, rows - kr)
        k_blk = lax.dynamic_slice_in_dim(kg, row_start, kr, axis=2)
        v_blk = lax.dynamic_slice_in_dim(vg, row_start, kr, axis=2)
        s = jnp.einsum('bhcd,bhrwd->bhcrw', q_row, k_blk).astype(jnp.float32) * scale
        dr_idx = row_start + jnp.arange(kr) - r + NA_WIN_ROWS - 1
        bias = rpb32[:, dr_idx[None, :, None], dc_idx[:, None, :]]
        s = jnp.where(col_mask[:, None, :], s + bias, -jnp.inf)
        p = jax.nn.softmax(s.reshape(B, NA_HEADS, GRID_W, kr * GRID_W), axis=-1).reshape(s.shape)
        return jnp.einsum('bhcrw,bhrwd->bhcd', p.astype(v_blk.dtype), v_blk)

    out = lax.map(one_row, (jnp.arange(rows), qg.transpose(2, 0, 1, 3, 4)))
    return out.transpose(1, 0, 3, 2, 4).reshape(B, T, NA_WIDTH)


def setup_inputs(seed: int = 0) -> dict:
    key = jax.random.key(seed)
    ks = jax.random.split(key, 20)
    f32 = jnp.float32

    def normal(k, shape, scale):
        return jax.random.normal(k, shape, f32) * scale

    def gain(k, shape):
        return 1.0 + 0.01 * jax.random.normal(k, shape, f32)

    return {
        "x": jax.random.normal(ks[0], (BATCH, SEQ, D_MODEL), f32),
        "ffn1_norm_pre": gain(ks[1], (DEPTH, D_MODEL)),
        "ffn1_w_gate": normal(ks[2], (DEPTH, D_MODEL, D_FF), D_MODEL ** -0.5),
        "ffn1_w_up": normal(ks[3], (DEPTH, D_MODEL, D_FF), D_MODEL ** -0.5),
        "ffn1_w_down": normal(ks[4], (DEPTH, D_FF, D_MODEL), D_FF ** -0.5),
        "ffn1_norm_post": gain(ks[5], (DEPTH, D_MODEL)),
        "mix_norm_pre": gain(ks[6], (DEPTH, D_MODEL)),
        "w_in": normal(ks[7], (DEPTH, D_MODEL, IN_COLS), D_MODEL ** -0.5),
        "hgrn_lb_logits": normal(ks[8], (DEPTH + 1, 2, HG_WIDTH), 1.0),
        "hgrn_head_norm": gain(ks[9], (DEPTH, HG_HEAD_DIM)),
        "na_rpb": normal(ks[10], (DEPTH, NA_HEADS, 2 * NA_WIN_ROWS - 1, 2 * NA_WIN_COLS - 1), 0.02),
        "w_out": normal(ks[11], (DEPTH, MIX_WIDTH, D_MODEL), MIX_WIDTH ** -0.5),
        "mix_norm_post": gain(ks[12], (DEPTH, D_MODEL)),
        "ffn2_norm_pre": gain(ks[13], (DEPTH, D_MODEL)),
        "ffn2_w_gate": normal(ks[14], (DEPTH, D_MODEL, D_FF), D_MODEL ** -0.5),
        "ffn2_w_up": normal(ks[15], (DEPTH, D_MODEL, D_FF), D_MODEL ** -0.5),
        "ffn2_w_down": normal(ks[16], (DEPTH, D_FF, D_MODEL), D_FF ** -0.5),
        "ffn2_norm_post": gain(ks[17], (DEPTH, D_MODEL)),
    }


def reference(x, ffn1_norm_pre, ffn1_w_gate, ffn1_w_up, ffn1_w_down, ffn1_norm_post,
              mix_norm_pre, w_in, hgrn_lb_logits, hgrn_head_norm, na_rpb, w_out, mix_norm_post,
              ffn2_norm_pre, ffn2_w_gate, ffn2_w_up, ffn2_w_down, ffn2_norm_post):
    h = x
    for layer in range(DEPTH):
        ff = swiglu(rmsnorm(h, ffn1_norm_pre[layer]), ffn1_w_gate[layer], ffn1_w_up[layer], ffn1_w_down[layer])
        h = h + FFN_RES_WEIGHT * rmsnorm(ff, ffn1_norm_post[layer])

        u = rmsnorm(h, mix_norm_pre[layer])
        proj = u @ w_in[layer]
        hq, hf_fwd, hf_bwd, hi, hg, nq, nk, nv = jnp.split(proj, IN_SPLITS, axis=-1)
        o_hg = hgrn2_bidirectional(hq, hf_fwd, hf_bwd, hi, hg, hgrn_lb_logits, layer, hgrn_head_norm[layer])
        o_na = neighbourhood_attention_2d(nq, nk, nv, na_rpb[layer])
        mixed = jnp.concatenate([o_hg, o_na], axis=-1) @ w_out[layer]
        h = h + rmsnorm(mixed, mix_norm_post[layer])

        ff = swiglu(rmsnorm(h, ffn2_norm_pre[layer]), ffn2_w_gate[layer], ffn2_w_up[layer], ffn2_w_down[layer])
        h = h + FFN_RES_WEIGHT * rmsnorm(ff, ffn2_norm_post[layer])
    return h
```

```python
import functools
import math

import numpy as np
import jax
import jax.numpy as jnp
from jax import lax
from jax.experimental import pallas as pl
from jax.experimental.pallas import tpu as pltpu

EPS = 1e-6
HEAD_DIM = 128
GRID_W = 64
NA_WIN_ROWS = 8
NA_WIN_COLS = 16
HG_CHUNK = 128
FFN_RES_WEIGHT = 0.5
NEG_BIG = -1e30
V7X_VMEM_LIMIT_BYTES = 56 * 1024 * 1024

BF16 = jnp.bfloat16
F32 = jnp.float32


def _params(*sem):
    return pltpu.CompilerParams(dimension_semantics=sem, vmem_limit_bytes=V7X_VMEM_LIMIT_BYTES)


def _dot(a, b):
    return jnp.dot(a, b, preferred_element_type=F32)


def _dot_nt(a, b):
    return lax.dot_general(a, b, (((1,), (1,)), ((), ())), preferred_element_type=F32)


def _dot_tn(a, b):
    return lax.dot_general(a, b, (((0,), (0,)), ((), ())), preferred_element_type=F32)


def _sigmoid(x):
    return 1.0 / (1.0 + jnp.exp(-x))


def _silu(x):
    return x * _sigmoid(x)


def _tile(n, pref, mult):
    t = (min(pref, n) // mult) * mult
    while t >= mult:
        if n % t == 0:
            return t
        t -= mult
    return n


def _rmsnorm_cast_kernel(x_ref, g_ref, o_ref):
    x = x_ref[...]
    inv = lax.rsqrt(jnp.mean(x * x, axis=-1, keepdims=True) + EPS)
    o_ref[...] = (x * inv * g_ref[...]).astype(o_ref.dtype)


def rmsnorm_cast(x, gain):
    m, d = x.shape
    tm = _tile(m, 256, 8)
    return pl.pallas_call(
        _rmsnorm_cast_kernel,
        grid=(m // tm,),
        in_specs=[pl.BlockSpec((tm, d), lambda i: (i, 0)), pl.BlockSpec((1, d), lambda i: (0, 0))],
        out_specs=pl.BlockSpec((tm, d), lambda i: (i, 0)),
        out_shape=jax.ShapeDtypeStruct((m, d), BF16),
        compiler_params=_params("parallel"),
        name="rmsnorm_cast",
    )(x, gain.reshape(1, d))


def _resnorm_kernel(h_ref, y_ref, gp_ref, gn_ref, hn_ref, u_ref, *, weight):
    y = y_ref[...]
    inv = lax.rsqrt(jnp.mean(y * y, axis=-1, keepdims=True) + EPS)
    h = h_ref[...] + weight * (y * inv * gp_ref[...])
    hn_ref[...] = h
    inv2 = lax.rsqrt(jnp.mean(h * h, axis=-1, keepdims=True) + EPS)
    u_ref[...] = (h * inv2 * gn_ref[...]).astype(u_ref.dtype)


def _resnorm_last_kernel(h_ref, y_ref, gp_ref, hn_ref, *, weight):
    y = y_ref[...]
    inv = lax.rsqrt(jnp.mean(y * y, axis=-1, keepdims=True) + EPS)
    hn_ref[...] = h_ref[...] + weight * (y * inv * gp_ref[...])


def resnorm(h, y, gain_post, weight, gain_next=None):
    m, d = h.shape
    tm = _tile(m, 256, 8)
    row = pl.BlockSpec((tm, d), lambda i: (i, 0))
    vec = pl.BlockSpec((1, d), lambda i: (0, 0))
    if gain_next is None:
        return pl.pallas_call(
            functools.partial(_resnorm_last_kernel, weight=weight),
            grid=(m // tm,),
            in_specs=[row, row, vec],
            out_specs=row,
            out_shape=jax.ShapeDtypeStruct((m, d), F32),
            compiler_params=_params("parallel"),
            name="resnorm_last",
        )(h, y, gain_post.reshape(1, d))
    return pl.pallas_call(
        functools.partial(_resnorm_kernel, weight=weight),
        grid=(m // tm,),
        in_specs=[row, row, vec, vec],
        out_specs=[row, row],
        out_shape=[jax.ShapeDtypeStruct((m, d), F32), jax.ShapeDtypeStruct((m, d), BF16)],
        compiler_params=_params("parallel"),
        name="resnorm",
    )(h, y, gain_post.reshape(1, d), gain_next.reshape(1, d))


def _gateup_kernel(x_ref, wg_ref, wu_ref, o_ref):
    x = x_ref[...]
    g = _dot(x, wg_ref[...])
    u = _dot(x, wu_ref[...])
    o_ref[...] = (_silu(g) * u).astype(o_ref.dtype)


def gateup(x, wg, wu):
    m, k = x.shape
    n = wg.shape[1]
    tm = _tile(m, 1024, 8)
    tn = _tile(n, 256, 128)
    return pl.pallas_call(
        _gateup_kernel,
        grid=(m // tm, n // tn),
        in_specs=[pl.BlockSpec((tm, k), lambda i, j: (i, 0)),
                  pl.BlockSpec((k, tn), lambda i, j: (0, j)),
                  pl.BlockSpec((k, tn), lambda i, j: (0, j))],
        out_specs=pl.BlockSpec((tm, tn), lambda i, j: (i, j)),
        out_shape=jax.ShapeDtypeStruct((m, n), BF16),
        compiler_params=_params("parallel", "arbitrary"),
        name="ffn_gateup",
    )(x, wg, wu)


def _mm_kernel(a_ref, b_ref, o_ref):
    o_ref[...] = _dot(a_ref[...], b_ref[...]).astype(o_ref.dtype)


def matmul(a, b, out_dtype, tm_pref, tn_pref, name):
    m, k = a.shape
    n = b.shape[1]
    tm = _tile(m, tm_pref, 8)
    tn = _tile(n, tn_pref, 128)
    return pl.pallas_call(
        _mm_kernel,
        grid=(m // tm, n // tn),
        in_specs=[pl.BlockSpec((tm, k), lambda i, j: (i, 0)),
                  pl.BlockSpec((k, tn), lambda i, j: (0, j))],
        out_specs=pl.BlockSpec((tm, tn), lambda i, j: (i, j)),
        out_shape=jax.ShapeDtypeStruct((m, n), out_dtype),
        compiler_params=_params("parallel", "arbitrary"),
        name=name,
    )(a, b)


def _mm2_kernel(a1_ref, a2_ref, b_ref, o_ref):
    k1 = a1_ref.shape[1]
    o_ref[...] = (_dot(a1_ref[...], b_ref[:k1, :]) + _dot(a2_ref[...], b_ref[k1:, :])).astype(o_ref.dtype)


def matmul_concat(a1, a2, b, out_dtype, tm_pref, tn_pref, name):
    m, k1 = a1.shape
    k2 = a2.shape[1]
    n = b.shape[1]
    tm = _tile(m, tm_pref, 8)
    tn = _tile(n, tn_pref, 128)
    return pl.pallas_call(
        _mm2_kernel,
        grid=(m // tm, n // tn),
        in_specs=[pl.BlockSpec((tm, k1), lambda i, j: (i, 0)),
                  pl.BlockSpec((tm, k2), lambda i, j: (i, 0)),
                  pl.BlockSpec((k1 + k2, tn), lambda i, j: (0, j))],
        out_specs=pl.BlockSpec((tm, tn), lambda i, j: (i, j)),
        out_shape=jax.ShapeDtypeStruct((m, n), out_dtype),
        compiler_params=_params("parallel", "arbitrary"),
        name=name,
    )(a1, a2, b)


def _hgrn_consts(chunk, reverse):
    nl = int(math.log2(chunk))
    assert 1 << nl == chunk
    t = np.arange(chunk)
    u = t[None, :]
    sums = np.zeros((nl + 2, chunk, chunk), np.float32)
    qside = np.zeros((nl, chunk), np.float32)
    amask = np.zeros((nl + 1, chunk, chunk), np.float32)
    for l in range(nl):
        half = 1 << l
        blk = t // (2 * half)
        mid = blk * 2 * half + half
        upper = t >= mid
        sums[l] = np.where(upper[:, None],
                           (u >= mid[:, None]) & (u <= t[:, None]),
                           (u > t[:, None]) & (u <= mid[:, None] - 1))
        qside[l] = upper
        amask[l] = (blk[:, None] == blk[None, :]) & upper[:, None] & ~upper[None, :]
    amask[nl] = np.eye(chunk)
    sums[nl] = u <= t[:, None]
    sums[nl + 1] = u > t[:, None]
    if reverse:
        sums = sums[:, ::-1, ::-1]
        qside = qside[:, ::-1]
        amask = amask[:, ::-1, ::-1]
    sums = np.ascontiguousarray(sums).reshape((nl + 2) * chunk, chunk)
    qside = np.ascontiguousarray(np.broadcast_to(qside[:, :, None], (nl, chunk, HEAD_DIM)))
    return (jnp.asarray(sums, BF16), jnp.asarray(qside, F32),
            jnp.asarray(np.ascontiguousarray(amask), F32))


def _hgrn_kernel(hq_ref, ff_ref, fb_ref, hi_ref, hg_ref, lb_ref, hn_ref,
                 sums_f_ref, sums_b_ref, qs_f_ref, qs_b_ref, am_f_ref, am_b_ref,
                 out_ref, of_ref, ob_ref, stf_ref, stb_ref, *, chunk, nl):
    seq = hq_ref.shape[0]
    nchunks = seq // chunk
    stf_ref[...] = jnp.zeros_like(stf_ref)
    stb_ref[...] = jnp.zeros_like(stb_ref)

    def chunk_step(base, direction, f_ref, sums_ref, qs_ref, am_ref, st_ref, o_ref):
        rows = pl.ds(pl.multiple_of(base, chunk), chunk)
        q = _silu(hq_ref[rows, :])
        lb = lb_ref[direction:direction + 1, :]
        f = lb + (1.0 - lb) * _sigmoid(f_ref[rows, :])
        k = 1.0 - f
        g = jnp.log(f)
        v = hi_ref[rows, :].astype(BF16)
        g1 = g.astype(BF16)
        r1 = g - g1.astype(F32)
        g2 = r1.astype(BF16)
        g3 = (r1 - g2.astype(F32)).astype(BF16)
        sums = sums_ref[...]
        e = jnp.exp(_dot(sums, g1) + _dot(sums, g2) + _dot(sums, g3))
        a = am_ref[nl] * _dot_nt(q.astype(BF16), k.astype(BF16))
        for l in range(nl):
            x = (jnp.where(qs_ref[l] > 0.5, q, k) * e[l * chunk:(l + 1) * chunk]).astype(BF16)
            a = a + am_ref[l] * _dot_nt(x, x)
        e_q = e[nl * chunk:(nl + 1) * chunk]
        qd = (q * e_q).astype(BF16)
        kd = (k * e[(nl + 1) * chunk:]).astype(BF16)
        st = st_ref[...]
        o_ref[rows, :] = _dot(a.astype(BF16), v) + _dot_nt(qd, st.astype(BF16))
        last = 0 if direction else chunk - 1
        st_ref[...] = st * e_q[last:last + 1, :] + _dot_tn(v, kd)

    def body(n, carry):
        chunk_step(n * chunk, 0, ff_ref, sums_f_ref, qs_f_ref, am_f_ref, stf_ref, of_ref)
        chunk_step((nchunks - 1 - n) * chunk, 1, fb_ref, sums_b_ref, qs_b_ref, am_b_ref, stb_ref, ob_ref)
        return carry

    lax.fori_loop(0, nchunks, body, 0)

    def finish(n, carry):
        rows = pl.ds(pl.multiple_of(n * chunk, chunk), chunk)
        o = of_ref[rows, :] + ob_ref[rows, :]
        inv = lax.rsqrt(jnp.mean(o * o, axis=-1, keepdims=True) + EPS)
        out_ref[rows, :] = (o * inv * hn_ref[...] * _silu(hg_ref[rows, :])).astype(out_ref.dtype)
        return carry

    lax.fori_loop(0, nchunks, finish, 0)


def hgrn2(proj, lb, head_norm, n_heads):
    b, t, _ = proj.shape
    chunk = min(HG_CHUNK, t)
    nl = int(math.log2(chunk))
    consts_f = _hgrn_consts(chunk, False)
    consts_b = _hgrn_consts(chunk, True)

    def col(group):
        return pl.BlockSpec((None, t, HEAD_DIM), lambda bi, hi, g=group: (bi, 0, g * n_heads + hi))

    def const(arr):
        return pl.BlockSpec(arr.shape, lambda bi, hi, nd=arr.ndim: (0,) * nd)

    return pl.pallas_call(
        functools.partial(_hgrn_kernel, chunk=chunk, nl=nl),
        grid=(b, n_heads),
        in_specs=[col(0), col(1), col(2), col(3), col(4),
                  pl.BlockSpec((2, HEAD_DIM), lambda bi, hi: (0, hi)),
                  pl.BlockSpec((1, HEAD_DIM), lambda bi, hi: (0, 0)),
                  const(consts_f[0]), const(consts_b[0]),
                  const(consts_f[1]), const(consts_b[1]),
                  const(consts_f[2]), const(consts_b[2])],
        out_specs=pl.BlockSpec((None, t, HEAD_DIM), lambda bi, hi: (bi, 0, hi)),
        out_shape=jax.ShapeDtypeStruct((b, t, n_heads * HEAD_DIM), BF16),
        scratch_shapes=[pltpu.VMEM((t, HEAD_DIM), F32), pltpu.VMEM((t, HEAD_DIM), F32),
                        pltpu.VMEM((HEAD_DIM, HEAD_DIM), F32), pltpu.VMEM((HEAD_DIM, HEAD_DIM), F32)],
        compiler_params=_params("parallel", "parallel"),
        name="hgrn2",
    )(proj, proj, proj, proj, proj, lb, head_norm.reshape(1, HEAD_DIM),
      consts_f[0], consts_b[0], consts_f[1], consts_b[1], consts_f[2], consts_b[2])


def _na_bias(rpb):
    col = np.arange(GRID_W)
    col_start = np.clip(col - NA_WIN_COLS // 2, 0, GRID_W - NA_WIN_COLS)
    col_mask = (col[None, :] >= col_start[:, None]) & (col[None, :] < col_start[:, None] + NA_WIN_COLS)
    dc_idx = np.clip(col[None, :] - col[:, None] + NA_WIN_COLS - 1, 0, 2 * NA_WIN_COLS - 2)
    dr_idx = np.arange(NA_WIN_ROWS)[:, None] + np.arange(NA_WIN_ROWS)[None, :]
    bias = rpb.astype(F32)[:, dr_idx[:, None, :, None], dc_idx[None, :, None, :]]
    bias = jnp.where(col_mask[None, None, :, None, :], bias, NEG_BIG)
    return bias.reshape(rpb.shape[0], NA_WIN_ROWS, GRID_W, NA_WIN_ROWS * GRID_W)


def _na_kernel(q_ref, k_ref, v_ref, bias_ref, out_ref, kb_ref, vb_ref, *, scale):
    seq = q_ref.shape[0]
    n_rows = seq // GRID_W
    slab = NA_WIN_ROWS * GRID_W
    kb_ref[...] = k_ref[...].astype(BF16)
    vb_ref[...] = v_ref[...].astype(BF16)

    def body(r, carry):
        row_start = jnp.clip(r - NA_WIN_ROWS // 2, 0, n_rows - NA_WIN_ROWS)
        offset = row_start - r + NA_WIN_ROWS - 1
        qrows = pl.ds(pl.multiple_of(r * GRID_W, GRID_W), GRID_W)
        krows = pl.ds(pl.multiple_of(row_start * GRID_W, GRID_W), slab)
        q = (q_ref[qrows, :] * scale).astype(BF16)
        s = _dot_nt(q, kb_ref[krows, :]) + bias_ref[offset]
        m = jnp.max(s, axis=-1, keepdims=True)
        p = jnp.exp(s - m)
        denom = jnp.sum(p, axis=-1, keepdims=True)
        o = _dot(p.astype(BF16), vb_ref[krows, :]) / denom
        out_ref[qrows, :] = o.astype(out_ref.dtype)
        return carry

    lax.fori_loop(0, n_rows, body, 0)


def neighbourhood_attention(proj, rpb, n_heads, col0):
    b, t, _ = proj.shape
    assert t % GRID_W == 0 and t // GRID_W >= NA_WIN_ROWS
    bias = _na_bias(rpb)

    def col(group):
        return pl.BlockSpec((None, t, HEAD_DIM), lambda bi, hi, g=group: (bi, 0, col0 + g * n_heads + hi))

    return pl.pallas_call(
        functools.partial(_na_kernel, scale=HEAD_DIM ** -0.5),
        grid=(b, n_heads),
        in_specs=[col(0), col(1), col(2),
                  pl.BlockSpec((None,) + bias.shape[1:], lambda bi, hi: (hi, 0, 0, 0))],
        out_specs=pl.BlockSpec((None, t, HEAD_DIM), lambda bi, hi: (bi, 0, hi)),
        out_shape=jax.ShapeDtypeStruct((b, t, n_heads * HEAD_DIM), BF16),
        scratch_shapes=[pltpu.VMEM((t, HEAD_DIM), BF16), pltpu.VMEM((t, HEAD_DIM), BF16)],
        compiler_params=_params("parallel", "parallel"),
        name="neighbourhood_attention",
    )(proj, proj, proj, bias)


def _swiglu_ffn(u, w_gate, w_up, w_down):
    act = gateup(u, w_gate.astype(BF16), w_up.astype(BF16))
    return matmul(act, w_down.astype(BF16), F32, 512, 256, "ffn_down")


def kernel(x, ffn1_norm_pre, ffn1_w_gate, ffn1_w_up, ffn1_w_down, ffn1_norm_post, mix_norm_pre, w_in, hgrn_lb_logits, hgrn_head_norm, na_rpb, w_out, mix_norm_post, ffn2_norm_pre, ffn2_w_gate, ffn2_w_up, ffn2_w_down, ffn2_norm_post):
    b, t, d = x.shape
    depth = w_in.shape[0]
    hg_width = hgrn_lb_logits.shape[-1]
    hg_heads = hg_width // HEAD_DIM
    na_heads = na_rpb.shape[1]
    h = x.reshape(b * t, d)
    u = rmsnorm_cast(h, ffn1_norm_pre[0])
    for layer in range(depth):
        ff = _swiglu_ffn(u, ffn1_w_gate[layer], ffn1_w_up[layer], ffn1_w_down[layer])
        h, u = resnorm(h, ff, ffn1_norm_post[layer], FFN_RES_WEIGHT, mix_norm_pre[layer])

        proj = matmul(u, w_in[layer].astype(BF16), F32, 1024, 512, "in_proj").reshape(b, t, -1)
        lb = jnp.cumsum(jax.nn.softmax(hgrn_lb_logits.astype(F32), axis=0), axis=0)[layer]
        o_hg = hgrn2(proj, lb, hgrn_head_norm[layer], hg_heads)
        o_na = neighbourhood_attention(proj, na_rpb[layer], na_heads, 5 * hg_heads)
        mixed = matmul_concat(o_hg.reshape(b * t, -1), o_na.reshape(b * t, -1), w_out[layer].astype(BF16),
                              F32, 1024, 512, "out_proj")
        h, u = resnorm(h, mixed, mix_norm_post[layer], 1.0, ffn2_norm_pre[layer])

        ff = _swiglu_ffn(u, ffn2_w_gate[layer], ffn2_w_up[layer], ffn2_w_down[layer])
        if layer + 1 < depth:
            h, u = resnorm(h, ff, ffn2_norm_post[layer], FFN_RES_WEIGHT, ffn1_norm_pre[layer + 1])
        else:
            h = resnorm(h, ff, ffn2_norm_post[layer], FFN_RES_WEIGHT)
    return h.reshape(b, t, d)
```

```python
import functools
import math

import numpy as np
import jax
import jax.numpy as jnp
from jax import lax
from jax.experimental import pallas as pl
from jax.experimental.pallas import tpu as pltpu

EPS = 1e-6
HEAD_DIM = 128
GRID_W = 64
NA_WIN_ROWS = 8
NA_WIN_COLS = 16
HG_CHUNK = 128
HG_TABLE_LEVELS = (1, 2)
HG_CHUNK_UNROLL = 2
NA_ROW_UNROLL = 4
FFN_RES_WEIGHT = 0.5
NEG_BIG = -1e30
V7X_VMEM_LIMIT_BYTES = 56 * 1024 * 1024

BF16 = jnp.bfloat16
F32 = jnp.float32


def _params(*sem):
    return pltpu.CompilerParams(dimension_semantics=sem, vmem_limit_bytes=V7X_VMEM_LIMIT_BYTES)


def _dot(a, b):
    return jnp.dot(a, b, preferred_element_type=F32)


def _dot_nt(a, b):
    return lax.dot_general(a, b, (((1,), (1,)), ((), ())), preferred_element_type=F32)


def _dot_tn(a, b):
    return lax.dot_general(a, b, (((0,), (0,)), ((), ())), preferred_element_type=F32)


def _sigmoid(x):
    return 1.0 / (1.0 + jnp.exp(-x))


def _silu(x):
    return x * _sigmoid(x)


def _tile(n, pref, mult):
    t = (min(pref, n) // mult) * mult
    while t >= mult:
        if n % t == 0:
            return t
        t -= mult
    return n


def _rmsnorm_cast_kernel(x_ref, g_ref, o_ref):
    x = x_ref[...]
    inv = lax.rsqrt(jnp.mean(x * x, axis=-1, keepdims=True) + EPS)
    o_ref[...] = (x * inv * g_ref[...]).astype(o_ref.dtype)


def rmsnorm_cast(x, gain):
    m, d = x.shape
    tm = _tile(m, 256, 8)
    return pl.pallas_call(
        _rmsnorm_cast_kernel,
        grid=(m // tm,),
        in_specs=[pl.BlockSpec((tm, d), lambda i: (i, 0)), pl.BlockSpec((1, d), lambda i: (0, 0))],
        out_specs=pl.BlockSpec((tm, d), lambda i: (i, 0)),
        out_shape=jax.ShapeDtypeStruct((m, d), BF16),
        compiler_params=_params("parallel"),
        name="rmsnorm_cast",
    )(x, gain.reshape(1, d))


def _resnorm_kernel(h_ref, y_ref, gp_ref, gn_ref, hn_ref, u_ref, *, weight):
    y = y_ref[...]
    inv = lax.rsqrt(jnp.mean(y * y, axis=-1, keepdims=True) + EPS)
    h = h_ref[...] + weight * (y * inv * gp_ref[...])
    hn_ref[...] = h
    inv2 = lax.rsqrt(jnp.mean(h * h, axis=-1, keepdims=True) + EPS)
    u_ref[...] = (h * inv2 * gn_ref[...]).astype(u_ref.dtype)


def _resnorm_last_kernel(h_ref, y_ref, gp_ref, hn_ref, *, weight):
    y = y_ref[...]
    inv = lax.rsqrt(jnp.mean(y * y, axis=-1, keepdims=True) + EPS)
    hn_ref[...] = h_ref[...] + weight * (y * inv * gp_ref[...])


def resnorm(h, y, gain_post, weight, gain_next=None):
    m, d = h.shape
    tm = _tile(m, 256, 8)
    row = pl.BlockSpec((tm, d), lambda i: (i, 0))
    vec = pl.BlockSpec((1, d), lambda i: (0, 0))
    if gain_next is None:
        return pl.pallas_call(
            functools.partial(_resnorm_last_kernel, weight=weight),
            grid=(m // tm,),
            in_specs=[row, row, vec],
            out_specs=row,
            out_shape=jax.ShapeDtypeStruct((m, d), F32),
            compiler_params=_params("parallel"),
            name="resnorm_last",
        )(h, y, gain_post.reshape(1, d))
    return pl.pallas_call(
        functools.partial(_resnorm_kernel, weight=weight),
        grid=(m // tm,),
        in_specs=[row, row, vec, vec],
        out_specs=[row, row],
        out_shape=[jax.ShapeDtypeStruct((m, d), F32), jax.ShapeDtypeStruct((m, d), BF16)],
        compiler_params=_params("parallel"),
        name="resnorm",
    )(h, y, gain_post.reshape(1, d), gain_next.reshape(1, d))


def _gateup_kernel(x_ref, wg_ref, wu_ref, o_ref):
    x = x_ref[...]
    g = _dot(x, wg_ref[...])
    u = _dot(x, wu_ref[...])
    o_ref[...] = (_silu(g) * u).astype(o_ref.dtype)


def gateup(x, wg, wu):
    m, k = x.shape
    n = wg.shape[1]
    tm = _tile(m, 1024, 8)
    tn = _tile(n, 256, 128)
    return pl.pallas_call(
        _gateup_kernel,
        grid=(m // tm, n // tn),
        in_specs=[pl.BlockSpec((tm, k), lambda i, j: (i, 0)),
                  pl.BlockSpec((k, tn), lambda i, j: (0, j)),
                  pl.BlockSpec((k, tn), lambda i, j: (0, j))],
        out_specs=pl.BlockSpec((tm, tn), lambda i, j: (i, j)),
        out_shape=jax.ShapeDtypeStruct((m, n), BF16),
        compiler_params=_params("parallel", "arbitrary"),
        name="ffn_gateup",
    )(x, wg, wu)


def _mm_kernel(a_ref, b_ref, o_ref):
    o_ref[...] = _dot(a_ref[...], b_ref[...]).astype(o_ref.dtype)


def matmul(a, b, out_dtype, tm_pref, tn_pref, name):
    m, k = a.shape
    n = b.shape[1]
    tm = _tile(m, tm_pref, 8)
    tn = _tile(n, tn_pref, 128)
    return pl.pallas_call(
        _mm_kernel,
        grid=(m // tm, n // tn),
        in_specs=[pl.BlockSpec((tm, k), lambda i, j: (i, 0)),
                  pl.BlockSpec((k, tn), lambda i, j: (0, j))],
        out_specs=pl.BlockSpec((tm, tn), lambda i, j: (i, j)),
        out_shape=jax.ShapeDtypeStruct((m, n), out_dtype),
        compiler_params=_params("parallel", "arbitrary"),
        name=name,
    )(a, b)


def _mm2_kernel(a1_ref, a2_ref, b_ref, o_ref):
    k1 = a1_ref.shape[1]
    o_ref[...] = (_dot(a1_ref[...], b_ref[:k1, :]) + _dot(a2_ref[...], b_ref[k1:, :])).astype(o_ref.dtype)


def matmul_concat(a1, a2, b, out_dtype, tm_pref, tn_pref, name):
    m, k1 = a1.shape
    k2 = a2.shape[1]
    n = b.shape[1]
    tm = _tile(m, tm_pref, 8)
    tn = _tile(n, tn_pref, 128)
    return pl.pallas_call(
        _mm2_kernel,
        grid=(m // tm, n // tn),
        in_specs=[pl.BlockSpec((tm, k1), lambda i, j: (i, 0)),
                  pl.BlockSpec((tm, k2), lambda i, j: (i, 0)),
                  pl.BlockSpec((k1 + k2, tn), lambda i, j: (0, j))],
        out_specs=pl.BlockSpec((tm, tn), lambda i, j: (i, j)),
        out_shape=jax.ShapeDtypeStruct((m, n), out_dtype),
        compiler_params=_params("parallel", "arbitrary"),
        name=name,
    )(a1, a2, b)


def _hgrn_consts(chunk, reverse):
    nl = int(math.log2(chunk))
    assert 1 << nl == chunk
    t = np.arange(chunk)
    u = t[None, :]
    level_sums = np.zeros((nl, chunk, chunk), np.float32)
    qside = np.zeros((nl, chunk), np.float32)
    amask = np.zeros((nl + 1, chunk, chunk), np.float32)
    for l in range(nl):
        half = 1 << l
        blk = t // (2 * half)
        mid = blk * 2 * half + half
        upper = t >= mid
        level_sums[l] = np.where(upper[:, None],
                                 (u >= mid[:, None]) & (u <= t[:, None]),
                                 (u > t[:, None]) & (u <= mid[:, None] - 1))
        qside[l] = upper
        amask[l] = (blk[:, None] == blk[None, :]) & upper[:, None] & ~upper[None, :]
    amask[nl] = np.eye(chunk)
    sums = np.stack([level_sums[l] for l in HG_TABLE_LEVELS] + [(u <= t[:, None]).astype(np.float32)])
    if reverse:
        sums = sums[:, ::-1, ::-1]
        qside = qside[:, ::-1]
        amask = amask[:, ::-1, ::-1]
    sums = np.ascontiguousarray(sums).reshape(len(sums) * chunk, chunk)
    qside = np.ascontiguousarray(np.broadcast_to(qside[:, :, None], (nl, chunk, HEAD_DIM)))
    return (jnp.asarray(sums, BF16), jnp.asarray(qside, F32),
            jnp.asarray(np.ascontiguousarray(amask), F32))


def _block_reference(cum, level, reverse):
    chunk = cum.shape[0]
    half = 1 << level
    pieces = []
    for start in range(0, chunk, 2 * half):
        row = start + half if reverse else start + half - 1
        pieces.append(jnp.broadcast_to(cum[row:row + 1, :], (2 * half, cum.shape[1])))
    return pieces[0] if len(pieces) == 1 else jnp.concatenate(pieces, axis=0)


def _hgrn_kernel(hq_ref, ff_ref, fb_ref, hi_ref, hg_ref, lb_ref, hn_ref,
                 sums_f_ref, sums_b_ref, qs_f_ref, qs_b_ref, am_f_ref, am_b_ref,
                 out_ref, of_ref, ob_ref, stf_ref, stb_ref, *, chunk, nl):
    seq = hq_ref.shape[0]
    nchunks = seq // chunk
    stf_ref[...] = jnp.zeros_like(stf_ref)
    stb_ref[...] = jnp.zeros_like(stb_ref)

    def chunk_step(base, direction, f_ref, sums_ref, qs_ref, am_ref, st_ref, o_ref):
        rows = pl.ds(pl.multiple_of(base, chunk), chunk)
        q = _silu(hq_ref[rows, :])
        lb = lb_ref[direction:direction + 1, :]
        f = lb + (1.0 - lb) * _sigmoid(f_ref[rows, :])
        k = 1.0 - f
        g = jnp.log(f)
        v = hi_ref[rows, :].astype(BF16)
        g1 = g.astype(BF16)
        g2 = (g - g1.astype(F32)).astype(BF16)
        sums = sums_ref[...]
        z = _dot(sums, g1) + _dot(sums, g2)
        cum = z[len(HG_TABLE_LEVELS) * chunk:]
        a = am_ref[nl] * _dot_nt(q.astype(BF16), k.astype(BF16))
        for l in range(nl):
            qside = qs_ref[l] > 0.5
            if l == 0:
                x = jnp.where(qside, q * f, k)
            else:
                if l in HG_TABLE_LEVELS:
                    i = HG_TABLE_LEVELS.index(l)
                    zl = z[i * chunk:(i + 1) * chunk]
                else:
                    d = cum - _block_reference(cum, l, bool(direction))
                    zl = jnp.where(qside, d, -d)
                x = jnp.where(qside, q, k) * jnp.exp(zl)
            x = x.astype(BF16)
            a = a + am_ref[l] * _dot_nt(x, x)
        last = 0 if direction else chunk - 1
        cum_last = cum[last:last + 1, :]
        qd = (q * jnp.exp(cum)).astype(BF16)
        kd = (k * jnp.exp(cum_last - cum)).astype(BF16)
        st = st_ref[...]
        o_ref[rows, :] = _dot(a.astype(BF16), v) + _dot_nt(qd, st.astype(BF16))
        st_ref[...] = st * jnp.exp(cum_last) + _dot_tn(v, kd)

    def body(n, carry):
        chunk_step(n * chunk, 0, ff_ref, sums_f_ref, qs_f_ref, am_f_ref, stf_ref, of_ref)
        chunk_step((nchunks - 1 - n) * chunk, 1, fb_ref, sums_b_ref, qs_b_ref, am_b_ref, stb_ref, ob_ref)
        return carry

    lax.fori_loop(0, nchunks, body, 0, unroll=min(HG_CHUNK_UNROLL, nchunks))

    def finish(n, carry):
        rows = pl.ds(pl.multiple_of(n * chunk, chunk), chunk)
        o = of_ref[rows, :] + ob_ref[rows, :]
        inv = lax.rsqrt(jnp.mean(o * o, axis=-1, keepdims=True) + EPS)
        out_ref[rows, :] = (o * inv * hn_ref[...] * _silu(hg_ref[rows, :])).astype(out_ref.dtype)
        return carry

    lax.fori_loop(0, nchunks, finish, 0)


def hgrn2(proj, lb, head_norm, n_heads):
    b, t, _ = proj.shape
    chunk = min(HG_CHUNK, t)
    nl = int(math.log2(chunk))
    consts_f = _hgrn_consts(chunk, False)
    consts_b = _hgrn_consts(chunk, True)

    def col(group):
        return pl.BlockSpec((None, t, HEAD_DIM), lambda bi, hi, g=group: (bi, 0, g * n_heads + hi))

    def const(arr):
        return pl.BlockSpec(arr.shape, lambda bi, hi, nd=arr.ndim: (0,) * nd)

    return pl.pallas_call(
        functools.partial(_hgrn_kernel, chunk=chunk, nl=nl),
        grid=(b, n_heads),
        in_specs=[col(0), col(1), col(2), col(3), col(4),
                  pl.BlockSpec((2, HEAD_DIM), lambda bi, hi: (0, hi)),
                  pl.BlockSpec((1, HEAD_DIM), lambda bi, hi: (0, 0)),
                  const(consts_f[0]), const(consts_b[0]),
                  const(consts_f[1]), const(consts_b[1]),
                  const(consts_f[2]), const(consts_b[2])],
        out_specs=pl.BlockSpec((None, t, HEAD_DIM), lambda bi, hi: (bi, 0, hi)),
        out_shape=jax.ShapeDtypeStruct((b, t, n_heads * HEAD_DIM), BF16),
        scratch_shapes=[pltpu.VMEM((t, HEAD_DIM), F32), pltpu.VMEM((t, HEAD_DIM), F32),
                        pltpu.VMEM((HEAD_DIM, HEAD_DIM), F32), pltpu.VMEM((HEAD_DIM, HEAD_DIM), F32)],
        compiler_params=_params("parallel", "parallel"),
        name="hgrn2",
    )(proj, proj, proj, proj, proj, lb, head_norm.reshape(1, HEAD_DIM),
      consts_f[0], consts_b[0], consts_f[1], consts_b[1], consts_f[2], consts_b[2])


def _na_col_mask():
    col = np.arange(GRID_W)
    col_start = np.clip(col - NA_WIN_COLS // 2, 0, GRID_W - NA_WIN_COLS)
    inside = (col[None, :] >= col_start[:, None]) & (col[None, :] < col_start[:, None] + NA_WIN_COLS)
    return jnp.asarray(np.tile(np.where(inside, 0.0, NEG_BIG), (1, 128 // GRID_W)), F32)


def _na_build_bias(rpb_ref, cmask_ref, tab_ref, bias_ref):
    lane = lax.broadcasted_iota(jnp.int32, (GRID_W, 128), 1)
    first_half = lane < GRID_W
    back = NA_WIN_COLS - 1
    for dr in range(2 * NA_WIN_ROWS - 1):
        row = jnp.broadcast_to(rpb_ref[dr:dr + 1, :], (GRID_W, 128))
        lo = pltpu.roll(row, 128 - back, 1, stride=1, stride_axis=0)
        hi = pltpu.roll(row, GRID_W - back, 1, stride=1, stride_axis=0)
        tab_ref[dr] = jnp.where(first_half, lo, hi) + cmask_ref[...]
    for offset in range(NA_WIN_ROWS):
        for pair in range(NA_WIN_ROWS // 2):
            bias_ref[offset, :, pair * 128:(pair + 1) * 128] = jnp.where(
                first_half, tab_ref[offset + 2 * pair], tab_ref[offset + 2 * pair + 1])


def _na_kernel(q_ref, k_ref, v_ref, rpb_ref, cmask_ref, out_ref, kb_ref, vb_ref, tab_ref, bias_ref, *, scale):
    seq = q_ref.shape[0]
    n_rows = seq // GRID_W
    slab = NA_WIN_ROWS * GRID_W
    kb_ref[...] = k_ref[...].astype(BF16)
    vb_ref[...] = v_ref[...].astype(BF16)
    _na_build_bias(rpb_ref, cmask_ref, tab_ref, bias_ref)

    def body(r, carry):
        row_start = jnp.clip(r - NA_WIN_ROWS // 2, 0, n_rows - NA_WIN_ROWS)
        offset = row_start - r + NA_WIN_ROWS - 1
        qrows = pl.ds(pl.multiple_of(r * GRID_W, GRID_W), GRID_W)
        krows = pl.ds(pl.multiple_of(row_start * GRID_W, GRID_W), slab)
        q = (q_ref[qrows, :] * scale).astype(BF16)
        s = _dot_nt(q, kb_ref[krows, :]) + bias_ref[offset]
        m = jnp.max(s, axis=-1, keepdims=True)
        p = jnp.exp(s - m)
        denom = jnp.sum(p, axis=-1, keepdims=True)
        o = _dot(p.astype(BF16), vb_ref[krows, :]) / denom
        out_ref[qrows, :] = o.astype(out_ref.dtype)
        return carry

    lax.fori_loop(0, n_rows, body, 0, unroll=NA_ROW_UNROLL)


def neighbourhood_attention(proj, rpb, n_heads, col0):
    b, t, _ = proj.shape
    assert t % GRID_W == 0 and t // GRID_W >= NA_WIN_ROWS
    n_dr, n_dc = rpb.shape[1:]
    assert (n_dr, n_dc) == (2 * NA_WIN_ROWS - 1, 2 * NA_WIN_COLS - 1)
    rpb_pad = jnp.pad(rpb.astype(F32), ((0, 0), (0, 1), (0, 128 - n_dc)))
    cmask = _na_col_mask()

    def col(group):
        return pl.BlockSpec((None, t, HEAD_DIM), lambda bi, hi, g=group: (bi, 0, col0 + g * n_heads + hi))

    return pl.pallas_call(
        functools.partial(_na_kernel, scale=HEAD_DIM ** -0.5),
        grid=(b, n_heads),
        in_specs=[col(0), col(1), col(2),
                  pl.BlockSpec((None, n_dr + 1, 128), lambda bi, hi: (hi, 0, 0)),
                  pl.BlockSpec(cmask.shape, lambda bi, hi: (0, 0))],
        out_specs=pl.BlockSpec((None, t, HEAD_DIM), lambda bi, hi: (bi, 0, hi)),
        out_shape=jax.ShapeDtypeStruct((b, t, n_heads * HEAD_DIM), BF16),
        scratch_shapes=[pltpu.VMEM((t, HEAD_DIM), BF16), pltpu.VMEM((t, HEAD_DIM), BF16),
                        pltpu.VMEM((n_dr + 1, GRID_W, 128), F32),
                        pltpu.VMEM((NA_WIN_ROWS, GRID_W, NA_WIN_ROWS * GRID_W), F32)],
        compiler_params=_params("parallel", "parallel"),
        name="neighbourhood_attention",
    )(proj, proj, proj, rpb_pad, cmask)


def _swiglu_ffn(u, w_gate, w_up, w_down):
    act = gateup(u, w_gate.astype(BF16), w_up.astype(BF16))
    return matmul(act, w_down.astype(BF16), F32, 512, 256, "ffn_down")


def kernel(x, ffn1_norm_pre, ffn1_w_gate, ffn1_w_up, ffn1_w_down, ffn1_norm_post, mix_norm_pre, w_in, hgrn_lb_logits, hgrn_head_norm, na_rpb, w_out, mix_norm_post, ffn2_norm_pre, ffn2_w_gate, ffn2_w_up, ffn2_w_down, ffn2_norm_post):
    b, t, d = x.shape
    depth = w_in.shape[0]
    hg_width = hgrn_lb_logits.shape[-1]
    hg_heads = hg_width // HEAD_DIM
    na_heads = na_rpb.shape[1]
    h = x.reshape(b * t, d)
    u = rmsnorm_cast(h, ffn1_norm_pre[0])
    for layer in range(depth):
        ff = _swiglu_ffn(u, ffn1_w_gate[layer], ffn1_w_up[layer], ffn1_w_down[layer])
        h, u = resnorm(h, ff, ffn1_norm_post[layer], FFN_RES_WEIGHT, mix_norm_pre[layer])

        proj = matmul(u, w_in[layer].astype(BF16), F32, 1024, 512, "in_proj").reshape(b, t, -1)
        lb = jnp.cumsum(jax.nn.softmax(hgrn_lb_logits.astype(F32), axis=0), axis=0)[layer]
        o_hg = hgrn2(proj, lb, hgrn_head_norm[layer], hg_heads)
        o_na = neighbourhood_attention(proj, na_rpb[layer], na_heads, 5 * hg_heads)
        mixed = matmul_concat(o_hg.reshape(b * t, -1), o_na.reshape(b * t, -1), w_out[layer].astype(BF16),
                              F32, 1024, 512, "out_proj")
        h, u = resnorm(h, mixed, mix_norm_post[layer], 1.0, ffn2_norm_pre[layer])

        ff = _swiglu_ffn(u, ffn2_w_gate[layer], ffn2_w_up[layer], ffn2_w_down[layer])
        if layer + 1 < depth:
            h, u = resnorm(h, ff, ffn2_norm_post[layer], FFN_RES_WEIGHT, ffn1_norm_pre[layer + 1])
        else:
            h = resnorm(h, ff, ffn2_norm_post[layer], FFN_RES_WEIGHT)
    return h.reshape(b, t, d)
```

```python
import functools
import math

import numpy as np
import jax
import jax.numpy as jnp
from jax import lax
from jax.experimental import pallas as pl
from jax.experimental.pallas import tpu as pltpu

EPS = 1e-6
HEAD_DIM = 128
GRID_W = 64
NA_WIN_ROWS = 8
NA_WIN_COLS = 16
HG_CHUNK = 128
HG_TABLE_LEVELS = (1, 2)
HG_CHUNK_UNROLL = 4
NA_GROUP = 4
NA_SLAB_ROWS = NA_WIN_ROWS + NA_GROUP
NA_GROUP_UNROLL = 2
FFN_RES_WEIGHT = 0.5
NEG_BIG = -1e30
V7X_VMEM_LIMIT_BYTES = 56 * 1024 * 1024

BF16 = jnp.bfloat16
F32 = jnp.float32


def _params(*sem):
    return pltpu.CompilerParams(dimension_semantics=sem, vmem_limit_bytes=V7X_VMEM_LIMIT_BYTES)


def _dot(a, b):
    return jnp.dot(a, b, preferred_element_type=F32)


def _dot_nt(a, b):
    return lax.dot_general(a, b, (((1,), (1,)), ((), ())), preferred_element_type=F32)


def _dot_tn(a, b):
    return lax.dot_general(a, b, (((0,), (0,)), ((), ())), preferred_element_type=F32)


def _sigmoid(x):
    return 1.0 / (1.0 + jnp.exp(-x))


def _silu(x):
    return x * _sigmoid(x)


def _tile(n, pref, mult):
    t = (min(pref, n) // mult) * mult
    while t >= mult:
        if n % t == 0:
            return t
        t -= mult
    return n


def _rmsnorm_cast_kernel(x_ref, g_ref, o_ref):
    x = x_ref[...]
    inv = lax.rsqrt(jnp.mean(x * x, axis=-1, keepdims=True) + EPS)
    o_ref[...] = (x * inv * g_ref[...]).astype(o_ref.dtype)


def rmsnorm_cast(x, gain):
    m, d = x.shape
    tm = _tile(m, 256, 8)
    return pl.pallas_call(
        _rmsnorm_cast_kernel,
        grid=(m // tm,),
        in_specs=[pl.BlockSpec((tm, d), lambda i: (i, 0)), pl.BlockSpec((1, d), lambda i: (0, 0))],
        out_specs=pl.BlockSpec((tm, d), lambda i: (i, 0)),
        out_shape=jax.ShapeDtypeStruct((m, d), BF16),
        compiler_params=_params("parallel"),
        name="rmsnorm_cast",
    )(x, gain.reshape(1, d))


def _resnorm_kernel(h_ref, y_ref, gp_ref, gn_ref, hn_ref, u_ref, *, weight):
    y = y_ref[...]
    inv = lax.rsqrt(jnp.mean(y * y, axis=-1, keepdims=True) + EPS)
    h = h_ref[...] + weight * (y * inv * gp_ref[...])
    hn_ref[...] = h
    inv2 = lax.rsqrt(jnp.mean(h * h, axis=-1, keepdims=True) + EPS)
    u_ref[...] = (h * inv2 * gn_ref[...]).astype(u_ref.dtype)


def _resnorm_last_kernel(h_ref, y_ref, gp_ref, hn_ref, *, weight):
    y = y_ref[...]
    inv = lax.rsqrt(jnp.mean(y * y, axis=-1, keepdims=True) + EPS)
    hn_ref[...] = h_ref[...] + weight * (y * inv * gp_ref[...])


def resnorm(h, y, gain_post, weight, gain_next=None):
    m, d = h.shape
    tm = _tile(m, 256, 8)
    row = pl.BlockSpec((tm, d), lambda i: (i, 0))
    vec = pl.BlockSpec((1, d), lambda i: (0, 0))
    if gain_next is None:
        return pl.pallas_call(
            functools.partial(_resnorm_last_kernel, weight=weight),
            grid=(m // tm,),
            in_specs=[row, row, vec],
            out_specs=row,
            out_shape=jax.ShapeDtypeStruct((m, d), F32),
            compiler_params=_params("parallel"),
            name="resnorm_last",
        )(h, y, gain_post.reshape(1, d))
    return pl.pallas_call(
        functools.partial(_resnorm_kernel, weight=weight),
        grid=(m // tm,),
        in_specs=[row, row, vec, vec],
        out_specs=[row, row],
        out_shape=[jax.ShapeDtypeStruct((m, d), F32), jax.ShapeDtypeStruct((m, d), BF16)],
        compiler_params=_params("parallel"),
        name="resnorm",
    )(h, y, gain_post.reshape(1, d), gain_next.reshape(1, d))


def _gateup_kernel(x_ref, wg_ref, wu_ref, o_ref):
    x = x_ref[...]
    g = _dot(x, wg_ref[...])
    u = _dot(x, wu_ref[...])
    o_ref[...] = (_silu(g) * u).astype(o_ref.dtype)


def _lhs_spec(tm, k):
    return pl.BlockSpec((tm, k), lambda i, j: (i, 0), pipeline_mode=pl.Buffered(1))


def gateup(x, wg, wu):
    m, k = x.shape
    n = wg.shape[1]
    tm = _tile(m, 2048, 8)
    tn = _tile(n, 256, 128)
    return pl.pallas_call(
        _gateup_kernel,
        grid=(m // tm, n // tn),
        in_specs=[_lhs_spec(tm, k),
                  pl.BlockSpec((k, tn), lambda i, j: (0, j)),
                  pl.BlockSpec((k, tn), lambda i, j: (0, j))],
        out_specs=pl.BlockSpec((tm, tn), lambda i, j: (i, j)),
        out_shape=jax.ShapeDtypeStruct((m, n), BF16),
        compiler_params=_params("parallel", "arbitrary"),
        name="ffn_gateup",
    )(x, wg, wu)


def _mm_kernel(a_ref, b_ref, o_ref):
    o_ref[...] = _dot(a_ref[...], b_ref[...]).astype(o_ref.dtype)


def matmul(a, b, out_dtype, tm_pref, tn_pref, name):
    m, k = a.shape
    n = b.shape[1]
    tm = _tile(m, tm_pref, 8)
    tn = _tile(n, tn_pref, 128)
    return pl.pallas_call(
        _mm_kernel,
        grid=(m // tm, n // tn),
        in_specs=[_lhs_spec(tm, k),
                  pl.BlockSpec((k, tn), lambda i, j: (0, j))],
        out_specs=pl.BlockSpec((tm, tn), lambda i, j: (i, j)),
        out_shape=jax.ShapeDtypeStruct((m, n), out_dtype),
        compiler_params=_params("parallel", "arbitrary"),
        name=name,
    )(a, b)


def _mm2_kernel(a1_ref, a2_ref, b_ref, o_ref):
    k1 = a1_ref.shape[1]
    o_ref[...] = (_dot(a1_ref[...], b_ref[:k1, :]) + _dot(a2_ref[...], b_ref[k1:, :])).astype(o_ref.dtype)


def matmul_concat(a1, a2, b, out_dtype, tm_pref, tn_pref, name):
    m, k1 = a1.shape
    k2 = a2.shape[1]
    n = b.shape[1]
    tm = _tile(m, tm_pref, 8)
    tn = _tile(n, tn_pref, 128)
    return pl.pallas_call(
        _mm2_kernel,
        grid=(m // tm, n // tn),
        in_specs=[pl.BlockSpec((tm, k1), lambda i, j: (i, 0)),
                  pl.BlockSpec((tm, k2), lambda i, j: (i, 0)),
                  pl.BlockSpec((k1 + k2, tn), lambda i, j: (0, j))],
        out_specs=pl.BlockSpec((tm, tn), lambda i, j: (i, j)),
        out_shape=jax.ShapeDtypeStruct((m, n), out_dtype),
        compiler_params=_params("parallel", "arbitrary"),
        name=name,
    )(a1, a2, b)


def _hgrn_consts(chunk, reverse):
    nl = int(math.log2(chunk))
    assert 1 << nl == chunk
    t = np.arange(chunk)
    u = t[None, :]
    level_sums = np.zeros((nl, chunk, chunk), np.float32)
    qside = np.zeros((nl, chunk), np.float32)
    amask = np.zeros((nl + 1, chunk, chunk), np.float32)
    for l in range(nl):
        half = 1 << l
        blk = t // (2 * half)
        mid = blk * 2 * half + half
        upper = t >= mid
        level_sums[l] = np.where(upper[:, None],
                                 (u >= mid[:, None]) & (u <= t[:, None]),
                                 (u > t[:, None]) & (u <= mid[:, None] - 1))
        qside[l] = upper
        amask[l] = (blk[:, None] == blk[None, :]) & upper[:, None] & ~upper[None, :]
    amask[nl] = np.eye(chunk)
    sums = np.stack([level_sums[l] for l in HG_TABLE_LEVELS] + [(u <= t[:, None]).astype(np.float32)])
    if reverse:
        sums = sums[:, ::-1, ::-1]
        qside = qside[:, ::-1]
        amask = amask[:, ::-1, ::-1]
    sums = np.ascontiguousarray(sums).reshape(len(sums) * chunk, chunk)
    sums = np.concatenate([sums, sums], axis=1)
    qside = 2.0 * qside - 1.0
    qside = np.ascontiguousarray(np.broadcast_to(qside[:, :, None], (nl, chunk, HEAD_DIM)))
    return (jnp.asarray(sums, BF16), jnp.asarray(qside, F32),
            jnp.asarray(np.ascontiguousarray(amask), F32))


def _block_reference(cum, level, reverse):
    chunk = cum.shape[0]
    half = 1 << level
    pieces = []
    for start in range(0, chunk, 2 * half):
        row = start + half if reverse else start + half - 1
        pieces.append(jnp.broadcast_to(cum[row:row + 1, :], (2 * half, cum.shape[1])))
    return pieces[0] if len(pieces) == 1 else jnp.concatenate(pieces, axis=0)


def _hgrn_kernel(hq_ref, ff_ref, fb_ref, hi_ref, hg_ref, lb_ref, hn_ref,
                 sums_f_ref, sums_b_ref, qs_f_ref, qs_b_ref, am_f_ref, am_b_ref,
                 out_ref, of_ref, ob_ref, stf_ref, stb_ref, *, chunk, nl):
    seq = hq_ref.shape[0]
    nchunks = seq // chunk
    stf_ref[...] = jnp.zeros_like(stf_ref)
    stb_ref[...] = jnp.zeros_like(stb_ref)

    def chunk_step(base, direction, f_ref, sums_ref, qs_ref, am_ref, st_ref, o_ref):
        rows = pl.ds(pl.multiple_of(base, chunk), chunk)
        q = _silu(hq_ref[rows, :])
        lb = lb_ref[direction:direction + 1, :]
        f = lb + (1.0 - lb) * _sigmoid(f_ref[rows, :])
        k = 1.0 - f
        g = jnp.log(f)
        v = hi_ref[rows, :].astype(BF16)
        g1 = g.astype(BF16)
        g2 = (g - g1.astype(F32)).astype(BF16)
        z = _dot(sums_ref[...], jnp.concatenate([g1, g2], axis=0))
        cum = z[len(HG_TABLE_LEVELS) * chunk:]
        a = am_ref[nl] * _dot_nt(q.astype(BF16), k.astype(BF16))
        for l in range(nl):
            side = qs_ref[l]
            qside = side > 0.0
            if l == 0:
                x = jnp.where(qside, q * f, k)
            else:
                if l in HG_TABLE_LEVELS:
                    i = HG_TABLE_LEVELS.index(l)
                    zl = z[i * chunk:(i + 1) * chunk]
                else:
                    zl = (cum - _block_reference(cum, l, bool(direction))) * side
                x = jnp.where(qside, q, k) * jnp.exp(zl)
            x = x.astype(BF16)
            a = a + am_ref[l] * _dot_nt(x, x)
        last = 0 if direction else chunk - 1
        cum_last = cum[last:last + 1, :]
        qd = (q * jnp.exp(cum)).astype(BF16)
        kd = (k * jnp.exp(cum_last - cum)).astype(BF16)
        st = st_ref[...]
        o_ref[rows, :] = _dot(a.astype(BF16), v) + _dot_nt(qd, st.astype(BF16))
        st_ref[...] = st * jnp.exp(cum_last) + _dot_tn(v, kd)

    def body(n, carry):
        chunk_step(n * chunk, 0, ff_ref, sums_f_ref, qs_f_ref, am_f_ref, stf_ref, of_ref)
        chunk_step((nchunks - 1 - n) * chunk, 1, fb_ref, sums_b_ref, qs_b_ref, am_b_ref, stb_ref, ob_ref)
        return carry

    lax.fori_loop(0, nchunks, body, 0, unroll=min(HG_CHUNK_UNROLL, nchunks))

    def finish(n, carry):
        rows = pl.ds(pl.multiple_of(n * chunk, chunk), chunk)
        o = of_ref[rows, :] + ob_ref[rows, :]
        inv = lax.rsqrt(jnp.mean(o * o, axis=-1, keepdims=True) + EPS)
        out_ref[rows, :] = (o * inv * hn_ref[...] * _silu(hg_ref[rows, :])).astype(out_ref.dtype)
        return carry

    lax.fori_loop(0, nchunks, finish, 0)


def hgrn2(proj, lb, head_norm, n_heads):
    b, t, _ = proj.shape
    chunk = min(HG_CHUNK, t)
    nl = int(math.log2(chunk))
    consts_f = _hgrn_consts(chunk, False)
    consts_b = _hgrn_consts(chunk, True)

    def col(group):
        return pl.BlockSpec((None, t, HEAD_DIM), lambda bi, hi, g=group: (bi, 0, g * n_heads + hi))

    def const(arr):
        return pl.BlockSpec(arr.shape, lambda bi, hi, nd=arr.ndim: (0,) * nd)

    return pl.pallas_call(
        functools.partial(_hgrn_kernel, chunk=chunk, nl=nl),
        grid=(b, n_heads),
        in_specs=[col(0), col(1), col(2), col(3), col(4),
                  pl.BlockSpec((2, HEAD_DIM), lambda bi, hi: (0, hi)),
                  pl.BlockSpec((1, HEAD_DIM), lambda bi, hi: (0, 0)),
                  const(consts_f[0]), const(consts_b[0]),
                  const(consts_f[1]), const(consts_b[1]),
                  const(consts_f[2]), const(consts_b[2])],
        out_specs=pl.BlockSpec((None, t, HEAD_DIM), lambda bi, hi: (bi, 0, hi)),
        out_shape=jax.ShapeDtypeStruct((b, t, n_heads * HEAD_DIM), BF16),
        scratch_shapes=[pltpu.VMEM((t, HEAD_DIM), F32), pltpu.VMEM((t, HEAD_DIM), F32),
                        pltpu.VMEM((HEAD_DIM, HEAD_DIM), F32), pltpu.VMEM((HEAD_DIM, HEAD_DIM), F32)],
        compiler_params=_params("parallel", "parallel"),
        name="hgrn2",
    )(proj, proj, proj, proj, proj, lb, head_norm.reshape(1, HEAD_DIM),
      consts_f[0], consts_b[0], consts_f[1], consts_b[1], consts_f[2], consts_b[2])


def _na_col_mask():
    col = np.arange(GRID_W)
    col_start = np.clip(col - NA_WIN_COLS // 2, 0, GRID_W - NA_WIN_COLS)
    inside = (col[None, :] >= col_start[:, None]) & (col[None, :] < col_start[:, None] + NA_WIN_COLS)
    return jnp.asarray(np.tile(np.where(inside, 0.0, NEG_BIG), (1, 128 // GRID_W)), F32)


def _na_build_bias(rpb_ref, cmask_ref, tab_ref, bias_ref):
    lane = lax.broadcasted_iota(jnp.int32, (GRID_W, 128), 1)
    first_half = lane < GRID_W
    back = NA_WIN_COLS - 1
    n_dr = 2 * NA_WIN_ROWS - 1
    for dr in range(n_dr):
        row = jnp.broadcast_to(rpb_ref[dr:dr + 1, :], (GRID_W, 128))
        lo = pltpu.roll(row, 128 - back, 1, stride=1, stride_axis=0)
        hi = pltpu.roll(row, GRID_W - back, 1, stride=1, stride_axis=0)
        tab_ref[dr] = jnp.where(first_half, lo, hi) + cmask_ref[...]
    tab_ref[n_dr] = jnp.full((GRID_W, 128), NEG_BIG, F32)
    for variant, delta in enumerate((0, -NA_GROUP, -2 * NA_GROUP)):
        for i in range(NA_GROUP):
            win_lo = (0, i, NA_GROUP)[variant]

            def tile(j):
                inside = win_lo <= j < win_lo + NA_WIN_ROWS
                return tab_ref[delta + j - i + NA_WIN_ROWS - 1 if inside else n_dr]

            for pair in range(NA_SLAB_ROWS // 2):
                bias_ref[variant, i * GRID_W:(i + 1) * GRID_W, pair * 128:(pair + 1) * 128] = jnp.where(
                    first_half, tile(2 * pair), tile(2 * pair + 1))


def _na_kernel(q_ref, k_ref, v_ref, rpb_ref, cmask_ref, out_ref, kb_ref, vb_ref, tab_ref, bias_ref, *, scale):
    seq = q_ref.shape[0]
    n_groups = seq // GRID_W // NA_GROUP
    group = NA_GROUP * GRID_W
    slab = NA_SLAB_ROWS * GRID_W
    kb_ref[...] = k_ref[...].astype(BF16)
    vb_ref[...] = v_ref[...].astype(BF16)
    _na_build_bias(rpb_ref, cmask_ref, tab_ref, bias_ref)

    def body(gi, carry):
        slab_start = jnp.clip(gi - 1, 0, n_groups - NA_SLAB_ROWS // NA_GROUP) * group
        variant = jnp.where(gi == 0, 0, jnp.where(gi == n_groups - 1, 2, 1))
        qrows = pl.ds(pl.multiple_of(gi * group, group), group)
        krows = pl.ds(pl.multiple_of(slab_start, group), slab)
        q = (q_ref[qrows, :] * scale).astype(BF16)
        s = _dot_nt(q, kb_ref[krows, :]) + bias_ref[variant]
        m = jnp.max(s, axis=-1, keepdims=True)
        p = jnp.exp(s - m)
        denom = jnp.sum(p, axis=-1, keepdims=True)
        o = _dot(p.astype(BF16), vb_ref[krows, :]) / denom
        out_ref[qrows, :] = o.astype(out_ref.dtype)
        return carry

    lax.fori_loop(0, n_groups, body, 0, unroll=NA_GROUP_UNROLL)


def neighbourhood_attention(proj, rpb, n_heads, col0):
    b, t, _ = proj.shape
    n_rows = t // GRID_W
    assert NA_GROUP == NA_WIN_ROWS // 2 and NA_SLAB_ROWS % NA_GROUP == 0 and (NA_SLAB_ROWS * GRID_W) % 128 == 0
    assert t % GRID_W == 0 and n_rows % NA_GROUP == 0 and n_rows >= NA_SLAB_ROWS + NA_GROUP
    n_dr, n_dc = rpb.shape[1:]
    assert (n_dr, n_dc) == (2 * NA_WIN_ROWS - 1, 2 * NA_WIN_COLS - 1)
    rpb_pad = jnp.pad(rpb.astype(F32), ((0, 0), (0, 1), (0, 128 - n_dc)))
    cmask = _na_col_mask()

    def col(group):
        return pl.BlockSpec((None, t, HEAD_DIM), lambda bi, hi, g=group: (bi, 0, col0 + g * n_heads + hi))

    return pl.pallas_call(
        functools.partial(_na_kernel, scale=HEAD_DIM ** -0.5),
        grid=(b, n_heads),
        in_specs=[col(0), col(1), col(2),
                  pl.BlockSpec((None, n_dr + 1, 128), lambda bi, hi: (hi, 0, 0)),
                  pl.BlockSpec(cmask.shape, lambda bi, hi: (0, 0))],
        out_specs=pl.BlockSpec((None, t, HEAD_DIM), lambda bi, hi: (bi, 0, hi)),
        out_shape=jax.ShapeDtypeStruct((b, t, n_heads * HEAD_DIM), BF16),
        scratch_shapes=[pltpu.VMEM((t, HEAD_DIM), BF16), pltpu.VMEM((t, HEAD_DIM), BF16),
                        pltpu.VMEM((n_dr + 1, GRID_W, 128), F32),
                        pltpu.VMEM((3, NA_GROUP * GRID_W, NA_SLAB_ROWS * GRID_W), F32)],
        compiler_params=_params("parallel", "parallel"),
        name="neighbourhood_attention",
    )(proj, proj, proj, rpb_pad, cmask)


def _swiglu_ffn(u, w_gate, w_up, w_down):
    act = gateup(u, w_gate.astype(BF16), w_up.astype(BF16))
    return matmul(act, w_down.astype(BF16), F32, 1024, 256, "ffn_down")


def kernel(x, ffn1_norm_pre, ffn1_w_gate, ffn1_w_up, ffn1_w_down, ffn1_norm_post, mix_norm_pre, w_in, hgrn_lb_logits, hgrn_head_norm, na_rpb, w_out, mix_norm_post, ffn2_norm_pre, ffn2_w_gate, ffn2_w_up, ffn2_w_down, ffn2_norm_post):
    b, t, d = x.shape
    depth = w_in.shape[0]
    hg_width = hgrn_lb_logits.shape[-1]
    hg_heads = hg_width // HEAD_DIM
    na_heads = na_rpb.shape[1]
    h = x.reshape(b * t, d)
    u = rmsnorm_cast(h, ffn1_norm_pre[0])
    for layer in range(depth):
        ff = _swiglu_ffn(u, ffn1_w_gate[layer], ffn1_w_up[layer], ffn1_w_down[layer])
        h, u = resnorm(h, ff, ffn1_norm_post[layer], FFN_RES_WEIGHT, mix_norm_pre[layer])

        proj = matmul(u, w_in[layer].astype(BF16), F32, 2048, 512, "in_proj").reshape(b, t, -1)
        lb = jnp.cumsum(jax.nn.softmax(hgrn_lb_logits.astype(F32), axis=0), axis=0)[layer]
        o_hg = hgrn2(proj, lb, hgrn_head_norm[layer], hg_heads)
        o_na = neighbourhood_attention(proj, na_rpb[layer], na_heads, 5 * hg_heads)
        mixed = matmul_concat(o_hg.reshape(b * t, -1), o_na.reshape(b * t, -1), w_out[layer].astype(BF16),
                              F32, 1024, 512, "out_proj")
        h, u = resnorm(h, mixed, mix_norm_post[layer], 1.0, ffn2_norm_pre[layer])

        ff = _swiglu_ffn(u, ffn2_w_gate[layer], ffn2_w_up[layer], ffn2_w_down[layer])
        if layer + 1 < depth:
            h, u = resnorm(h, ff, ffn2_norm_post[layer], FFN_RES_WEIGHT, ffn1_norm_pre[layer + 1])
        else:
            h = resnorm(h, ff, ffn2_norm_post[layer], FFN_RES_WEIGHT)
    return h.reshape(b, t, d)
```

```python
import functools
import math

import numpy as np
import jax
import jax.numpy as jnp
from jax import lax
from jax.experimental import pallas as pl
from jax.experimental.pallas import tpu as pltpu

EPS = 1e-6
HEAD_DIM = 128
GRID_W = 64
NA_WIN_ROWS = 8
NA_WIN_COLS = 16
HG_CHUNK = 128
HG_TABLE_LEVELS = (1, 2)
HG_CHUNK_UNROLL = 4
NA_GROUP = 4
NA_SLAB_ROWS = NA_WIN_ROWS + NA_GROUP
NA_GROUP_UNROLL = 2
FFN_RES_WEIGHT = 0.5
NEG_BIG = -1e30
V7X_VMEM_LIMIT_BYTES = 56 * 1024 * 1024

BF16 = jnp.bfloat16
F32 = jnp.float32


def _params(*sem):
    return pltpu.CompilerParams(dimension_semantics=sem, vmem_limit_bytes=V7X_VMEM_LIMIT_BYTES)


def _dot(a, b):
    return jnp.dot(a, b, preferred_element_type=F32)


def _dot_nt(a, b):
    return lax.dot_general(a, b, (((1,), (1,)), ((), ())), preferred_element_type=F32)


def _dot_tn(a, b):
    return lax.dot_general(a, b, (((0,), (0,)), ((), ())), preferred_element_type=F32)


def _sigmoid(x):
    return 1.0 / (1.0 + jnp.exp(-x))


def _silu(x):
    return x * _sigmoid(x)


def _tile(n, pref, mult):
    t = (min(pref, n) // mult) * mult
    while t >= mult:
        if n % t == 0:
            return t
        t -= mult
    return n


def _rmsnorm_cast_kernel(x_ref, g_ref, o_ref):
    x = x_ref[...]
    inv = lax.rsqrt(jnp.mean(x * x, axis=-1, keepdims=True) + EPS)
    o_ref[...] = (x * inv * g_ref[...]).astype(o_ref.dtype)


def rmsnorm_cast(x, gain):
    m, d = x.shape
    tm = _tile(m, 256, 8)
    return pl.pallas_call(
        _rmsnorm_cast_kernel,
        grid=(m // tm,),
        in_specs=[pl.BlockSpec((tm, d), lambda i: (i, 0)), pl.BlockSpec((1, d), lambda i: (0, 0))],
        out_specs=pl.BlockSpec((tm, d), lambda i: (i, 0)),
        out_shape=jax.ShapeDtypeStruct((m, d), BF16),
        compiler_params=_params("parallel"),
        name="rmsnorm_cast",
    )(x, gain.reshape(1, d))


def _resnorm_kernel(h_ref, y_ref, gp_ref, gn_ref, hn_ref, u_ref, *, weight):
    y = y_ref[...].astype(F32)
    inv = lax.rsqrt(jnp.mean(y * y, axis=-1, keepdims=True) + EPS)
    h = h_ref[...] + weight * (y * inv * gp_ref[...])
    hn_ref[...] = h
    inv2 = lax.rsqrt(jnp.mean(h * h, axis=-1, keepdims=True) + EPS)
    u_ref[...] = (h * inv2 * gn_ref[...]).astype(u_ref.dtype)


def _resnorm_last_kernel(h_ref, y_ref, gp_ref, hn_ref, *, weight):
    y = y_ref[...].astype(F32)
    inv = lax.rsqrt(jnp.mean(y * y, axis=-1, keepdims=True) + EPS)
    hn_ref[...] = h_ref[...] + weight * (y * inv * gp_ref[...])


def resnorm(h, y, gain_post, weight, gain_next=None):
    m, d = h.shape
    tm = _tile(m, 256, 8)
    row = pl.BlockSpec((tm, d), lambda i: (i, 0))
    vec = pl.BlockSpec((1, d), lambda i: (0, 0))
    if gain_next is None:
        return pl.pallas_call(
            functools.partial(_resnorm_last_kernel, weight=weight),
            grid=(m // tm,),
            in_specs=[row, row, vec],
            out_specs=row,
            out_shape=jax.ShapeDtypeStruct((m, d), F32),
            compiler_params=_params("parallel"),
            name="resnorm_last",
        )(h, y, gain_post.reshape(1, d))
    return pl.pallas_call(
        functools.partial(_resnorm_kernel, weight=weight),
        grid=(m // tm,),
        in_specs=[row, row, vec, vec],
        out_specs=[row, row],
        out_shape=[jax.ShapeDtypeStruct((m, d), F32), jax.ShapeDtypeStruct((m, d), BF16)],
        compiler_params=_params("parallel"),
        name="resnorm",
    )(h, y, gain_post.reshape(1, d), gain_next.reshape(1, d))


def _gateup_kernel(x_ref, wg_ref, wu_ref, o_ref):
    x = x_ref[...]
    g = _dot(x, wg_ref[...].astype(BF16))
    u = _dot(x, wu_ref[...].astype(BF16))
    o_ref[...] = (_silu(g) * u).astype(o_ref.dtype)


def gateup(x, wg, wu):
    m, k = x.shape
    n = wg.shape[1]
    tm = _tile(m, 1024, 8)
    tn = _tile(n, 256, 128)
    return pl.pallas_call(
        _gateup_kernel,
        grid=(m // tm, n // tn),
        in_specs=[pl.BlockSpec((tm, k), lambda i, j: (i, 0)),
                  pl.BlockSpec((k, tn), lambda i, j: (0, j)),
                  pl.BlockSpec((k, tn), lambda i, j: (0, j))],
        out_specs=pl.BlockSpec((tm, tn), lambda i, j: (i, j)),
        out_shape=jax.ShapeDtypeStruct((m, n), BF16),
        compiler_params=_params("parallel", "arbitrary"),
        name="ffn_gateup",
    )(x, wg, wu)


def _mm_kernel(a_ref, b_ref, o_ref):
    o_ref[...] = _dot(a_ref[...], b_ref[...].astype(BF16)).astype(o_ref.dtype)


def column_tiles(w, tn):
    k, n = w.shape
    return w.astype(BF16).reshape(k, n // tn, tn).transpose(1, 0, 2)


def matmul(a, b, out_dtype, tm_pref, tn_pref, name):
    m, k = a.shape
    tm = _tile(m, tm_pref, 8)
    if b.ndim == 3:
        n_tiles, _, tn = b.shape
        b_spec = pl.BlockSpec((None, k, tn), lambda i, j: (j, 0, 0))
    else:
        tn = _tile(b.shape[1], tn_pref, 128)
        n_tiles = b.shape[1] // tn
        b_spec = pl.BlockSpec((k, tn), lambda i, j: (0, j))
    a_spec = pl.BlockSpec((tm, k), lambda i, j: (i, 0), pipeline_mode=pl.Buffered(1))
    return pl.pallas_call(
        _mm_kernel,
        grid=(m // tm, n_tiles),
        in_specs=[a_spec, b_spec],
        out_specs=pl.BlockSpec((tm, tn), lambda i, j: (i, j)),
        out_shape=jax.ShapeDtypeStruct((m, n_tiles * tn), out_dtype),
        compiler_params=_params("parallel", "arbitrary"),
        name=name,
    )(a, b)


def _mm2_kernel(a1_ref, a2_ref, b_ref, o_ref):
    k1 = a1_ref.shape[1]
    o_ref[...] = (_dot(a1_ref[...], b_ref[:k1, :]) + _dot(a2_ref[...], b_ref[k1:, :])).astype(o_ref.dtype)


def matmul_concat(a1, a2, b, out_dtype, tm_pref, tn_pref, name):
    m, k1 = a1.shape
    k2 = a2.shape[1]
    n = b.shape[1]
    tm = _tile(m, tm_pref, 8)
    tn = _tile(n, tn_pref, 128)
    return pl.pallas_call(
        _mm2_kernel,
        grid=(m // tm, n // tn),
        in_specs=[pl.BlockSpec((tm, k1), lambda i, j: (i, 0)),
                  pl.BlockSpec((tm, k2), lambda i, j: (i, 0)),
                  pl.BlockSpec((k1 + k2, tn), lambda i, j: (0, j))],
        out_specs=pl.BlockSpec((tm, tn), lambda i, j: (i, j)),
        out_shape=jax.ShapeDtypeStruct((m, n), out_dtype),
        compiler_params=_params("parallel", "arbitrary"),
        name=name,
    )(a1, a2, b)


def _hgrn_consts(chunk, reverse):
    nl = int(math.log2(chunk))
    assert 1 << nl == chunk
    t = np.arange(chunk)
    u = t[None, :]
    level_sums = np.zeros((nl, chunk, chunk), np.float32)
    qside = np.zeros((nl, chunk), np.float32)
    amask = np.zeros((nl + 1, chunk, chunk), np.float32)
    for l in range(nl):
        half = 1 << l
        blk = t // (2 * half)
        mid = blk * 2 * half + half
        upper = t >= mid
        level_sums[l] = np.where(upper[:, None],
                                 (u >= mid[:, None]) & (u <= t[:, None]),
                                 (u > t[:, None]) & (u <= mid[:, None] - 1))
        qside[l] = upper
        amask[l] = (blk[:, None] == blk[None, :]) & upper[:, None] & ~upper[None, :]
    amask[nl] = np.eye(chunk)
    sums = np.stack([level_sums[l] for l in HG_TABLE_LEVELS] + [(u <= t[:, None]).astype(np.float32)])
    if reverse:
        sums = sums[:, ::-1, ::-1]
        qside = qside[:, ::-1]
        amask = amask[:, ::-1, ::-1]
    sums = np.ascontiguousarray(sums).reshape(len(sums) * chunk, chunk)
    sums = np.concatenate([sums, sums], axis=1)
    qside = 2.0 * qside - 1.0
    qside = np.ascontiguousarray(np.broadcast_to(qside[:, :, None], (nl, chunk, HEAD_DIM)))
    return (jnp.asarray(sums, BF16), jnp.asarray(qside, F32),
            jnp.asarray(np.ascontiguousarray(amask), F32))


def _block_reference(cum, level, reverse):
    chunk = cum.shape[0]
    half = 1 << level
    pieces = []
    for start in range(0, chunk, 2 * half):
        row = start + half if reverse else start + half - 1
        pieces.append(jnp.broadcast_to(cum[row:row + 1, :], (2 * half, cum.shape[1])))
    return pieces[0] if len(pieces) == 1 else jnp.concatenate(pieces, axis=0)


def _hgrn_kernel(hq_ref, ff_ref, fb_ref, hi_ref, hg_ref, lb_ref, hn_ref, *rest, chunk, nl):
    consts_f, consts_b = rest[:3], rest[3:6]
    out_ref, of_ref, ob_ref, stf_ref, stb_ref = rest[6:]
    seq = hq_ref.shape[0]
    nchunks = seq // chunk
    stf_ref[...] = jnp.zeros_like(stf_ref)
    stb_ref[...] = jnp.zeros_like(stb_ref)

    def chunk_step(base, direction, f_ref, consts, st_ref, o_ref):
        sums_ref, side_ref, am_ref = consts
        rows = pl.ds(pl.multiple_of(base, chunk), chunk)
        q = _silu(hq_ref[rows, :])
        lb = lb_ref[direction:direction + 1, :]
        f = lb + (1.0 - lb) * _sigmoid(f_ref[rows, :])
        k = 1.0 - f
        g = jnp.log(f)
        v = hi_ref[rows, :].astype(BF16)
        g1 = g.astype(BF16)
        g2 = (g - g1.astype(F32)).astype(BF16)
        z = _dot(sums_ref[...], jnp.concatenate([g1, g2], axis=0))
        cum = z[len(HG_TABLE_LEVELS) * chunk:]
        a = am_ref[nl] * _dot_nt(q.astype(BF16), k.astype(BF16))
        for l in range(nl):
            side = side_ref[l]
            qside = side > 0.0
            if l == 0:
                x = jnp.where(qside, q * f, k)
            else:
                if l in HG_TABLE_LEVELS:
                    i = HG_TABLE_LEVELS.index(l)
                    zl = z[i * chunk:(i + 1) * chunk]
                else:
                    zl = (cum - _block_reference(cum, l, bool(direction))) * side
                x = jnp.where(qside, q, k) * jnp.exp(zl)
            a = a + am_ref[l] * _dot_nt(x.astype(BF16), x.astype(BF16))
        last = 0 if direction else chunk - 1
        cum_last = cum[last:last + 1, :]
        qd = (q * jnp.exp(cum)).astype(BF16)
        kd = (k * jnp.exp(cum_last - cum)).astype(BF16)
        st = st_ref[...]
        o_ref[rows, :] = _dot(a.astype(BF16), v) + _dot_nt(qd, st.astype(BF16))
        st_ref[...] = st * jnp.exp(cum_last) + _dot_tn(v, kd)

    def body(n, carry):
        chunk_step(n * chunk, 0, ff_ref, consts_f, stf_ref, of_ref)
        chunk_step((nchunks - 1 - n) * chunk, 1, fb_ref, consts_b, stb_ref, ob_ref)
        return carry

    lax.fori_loop(0, nchunks, body, 0, unroll=min(HG_CHUNK_UNROLL, nchunks))

    def finish(n, carry):
        rows = pl.ds(pl.multiple_of(n * chunk, chunk), chunk)
        o = of_ref[rows, :] + ob_ref[rows, :]
        inv = lax.rsqrt(jnp.mean(o * o, axis=-1, keepdims=True) + EPS)
        out_ref[rows, :] = (o * inv * hn_ref[...] * _silu(hg_ref[rows, :])).astype(out_ref.dtype)
        return carry

    lax.fori_loop(0, nchunks, finish, 0)


def hgrn2(proj, lb, head_norm, n_heads):
    b, t, _ = proj.shape
    chunk = min(HG_CHUNK, t)
    nl = int(math.log2(chunk))
    consts_f = _hgrn_consts(chunk, False)
    consts_b = _hgrn_consts(chunk, True)

    def col(group):
        return pl.BlockSpec((None, t, HEAD_DIM), lambda bi, hi, g=group: (bi, 0, g * n_heads + hi))

    def const(arr):
        return pl.BlockSpec(arr.shape, lambda bi, hi, nd=arr.ndim: (0,) * nd)

    return pl.pallas_call(
        functools.partial(_hgrn_kernel, chunk=chunk, nl=nl),
        grid=(b, n_heads),
        in_specs=[col(0), col(1), col(2), col(3), col(4),
                  pl.BlockSpec((2, HEAD_DIM), lambda bi, hi: (0, hi)),
                  pl.BlockSpec((1, HEAD_DIM), lambda bi, hi: (0, 0)),
                  *[const(c) for c in consts_f], *[const(c) for c in consts_b]],
        out_specs=pl.BlockSpec((None, t, HEAD_DIM), lambda bi, hi: (bi, 0, hi)),
        out_shape=jax.ShapeDtypeStruct((b, t, n_heads * HEAD_DIM), BF16),
        scratch_shapes=[pltpu.VMEM((t, HEAD_DIM), F32), pltpu.VMEM((t, HEAD_DIM), F32),
                        pltpu.VMEM((HEAD_DIM, HEAD_DIM), F32), pltpu.VMEM((HEAD_DIM, HEAD_DIM), F32)],
        compiler_params=_params("parallel", "parallel"),
        name="hgrn2",
    )(proj, proj, proj, proj, proj, lb, head_norm.reshape(1, HEAD_DIM), *consts_f, *consts_b)


def _na_col_mask():
    col = np.arange(GRID_W)
    col_start = np.clip(col - NA_WIN_COLS // 2, 0, GRID_W - NA_WIN_COLS)
    inside = (col[None, :] >= col_start[:, None]) & (col[None, :] < col_start[:, None] + NA_WIN_COLS)
    return jnp.asarray(np.tile(np.where(inside, 0.0, NEG_BIG), (1, 128 // GRID_W)), F32)


def _na_build_bias(rpb_ref, cmask_ref, tab_ref, bias_ref):
    lane = lax.broadcasted_iota(jnp.int32, (GRID_W, 128), 1)
    first_half = lane < GRID_W
    back = NA_WIN_COLS - 1
    n_dr = 2 * NA_WIN_ROWS - 1
    for dr in range(n_dr):
        row = jnp.broadcast_to(rpb_ref[dr:dr + 1, :], (GRID_W, 128))
        lo = pltpu.roll(row, 128 - back, 1, stride=1, stride_axis=0)
        hi = pltpu.roll(row, GRID_W - back, 1, stride=1, stride_axis=0)
        tab_ref[dr] = jnp.where(first_half, lo, hi) + cmask_ref[...]
    tab_ref[n_dr] = jnp.full((GRID_W, 128), NEG_BIG, F32)
    for variant, delta in enumerate((0, -NA_GROUP, -2 * NA_GROUP)):
        for i in range(NA_GROUP):
            win_lo = (0, i, NA_GROUP)[variant]

            def tile(j):
                inside = win_lo <= j < win_lo + NA_WIN_ROWS
                return tab_ref[delta + j - i + NA_WIN_ROWS - 1 if inside else n_dr]

            for pair in range(NA_SLAB_ROWS // 2):
                bias_ref[variant, i * GRID_W:(i + 1) * GRID_W, pair * 128:(pair + 1) * 128] = jnp.where(
                    first_half, tile(2 * pair), tile(2 * pair + 1))


def _na_kernel(q_ref, k_ref, v_ref, rpb_ref, cmask_ref, out_ref, kb_ref, vb_ref, tab_ref, bias_ref, *, scale):
    seq = q_ref.shape[0]
    n_groups = seq // GRID_W // NA_GROUP
    group = NA_GROUP * GRID_W
    slab = NA_SLAB_ROWS * GRID_W
    kb_ref[...] = k_ref[...].astype(BF16)
    vb_ref[...] = v_ref[...].astype(BF16)
    _na_build_bias(rpb_ref, cmask_ref, tab_ref, bias_ref)

    def body(gi, carry):
        slab_start = jnp.clip(gi - 1, 0, n_groups - NA_SLAB_ROWS // NA_GROUP) * group
        variant = jnp.where(gi == 0, 0, jnp.where(gi == n_groups - 1, 2, 1))
        qrows = pl.ds(pl.multiple_of(gi * group, group), group)
        krows = pl.ds(pl.multiple_of(slab_start, group), slab)
        q = (q_ref[qrows, :] * scale).astype(BF16)
        s = _dot_nt(q, kb_ref[krows, :]) + bias_ref[variant]
        m = jnp.max(s, axis=-1, keepdims=True)
        p = jnp.exp(s - m)
        denom = jnp.sum(p, axis=-1, keepdims=True)
        o = _dot(p.astype(BF16), vb_ref[krows, :]) / denom
        out_ref[qrows, :] = o.astype(out_ref.dtype)
        return carry

    lax.fori_loop(0, n_groups, body, 0, unroll=NA_GROUP_UNROLL)


def neighbourhood_attention(proj, rpb, n_heads, col0):
    b, t, _ = proj.shape
    n_rows = t // GRID_W
    assert NA_GROUP == NA_WIN_ROWS // 2 and NA_SLAB_ROWS % NA_GROUP == 0 and (NA_SLAB_ROWS * GRID_W) % 128 == 0
    assert t % GRID_W == 0 and n_rows % NA_GROUP == 0 and n_rows >= NA_SLAB_ROWS + NA_GROUP
    n_dr, n_dc = rpb.shape[1:]
    assert (n_dr, n_dc) == (2 * NA_WIN_ROWS - 1, 2 * NA_WIN_COLS - 1)
    rpb_pad = jnp.pad(rpb.astype(F32), ((0, 0), (0, 1), (0, 128 - n_dc)))
    cmask = _na_col_mask()

    def col(group):
        return pl.BlockSpec((None, t, HEAD_DIM), lambda bi, hi, g=group: (bi, 0, col0 + g * n_heads + hi))

    return pl.pallas_call(
        functools.partial(_na_kernel, scale=HEAD_DIM ** -0.5),
        grid=(b, n_heads),
        in_specs=[col(0), col(1), col(2),
                  pl.BlockSpec((None, n_dr + 1, 128), lambda bi, hi: (hi, 0, 0)),
                  pl.BlockSpec(cmask.shape, lambda bi, hi: (0, 0))],
        out_specs=pl.BlockSpec((None, t, HEAD_DIM), lambda bi, hi: (bi, 0, hi)),
        out_shape=jax.ShapeDtypeStruct((b, t, n_heads * HEAD_DIM), BF16),
        scratch_shapes=[pltpu.VMEM((t, HEAD_DIM), BF16), pltpu.VMEM((t, HEAD_DIM), BF16),
                        pltpu.VMEM((n_dr + 1, GRID_W, 128), F32),
                        pltpu.VMEM((3, NA_GROUP * GRID_W, NA_SLAB_ROWS * GRID_W), F32)],
        compiler_params=_params("parallel", "parallel"),
        name="neighbourhood_attention",
    )(proj, proj, proj, rpb_pad, cmask)


def _swiglu_ffn(u, w_gate, w_up, w_down):
    act = gateup(u, w_gate, w_up)
    return matmul(act, column_tiles(w_down, 256), BF16, 1024, 256, "ffn_down")


def kernel(x, ffn1_norm_pre, ffn1_w_gate, ffn1_w_up, ffn1_w_down, ffn1_norm_post, mix_norm_pre, w_in, hgrn_lb_logits, hgrn_head_norm, na_rpb, w_out, mix_norm_post, ffn2_norm_pre, ffn2_w_gate, ffn2_w_up, ffn2_w_down, ffn2_norm_post):
    b, t, d = x.shape
    depth = w_in.shape[0]
    hg_width = hgrn_lb_logits.shape[-1]
    hg_heads = hg_width // HEAD_DIM
    na_heads = na_rpb.shape[1]
    h = x.reshape(b * t, d)
    u = rmsnorm_cast(h, ffn1_norm_pre[0])
    for layer in range(depth):
        ff = _swiglu_ffn(u, ffn1_w_gate[layer], ffn1_w_up[layer], ffn1_w_down[layer])
        h, u = resnorm(h, ff, ffn1_norm_post[layer], FFN_RES_WEIGHT, mix_norm_pre[layer])

        proj = matmul(u, w_in[layer], F32, 2048, 512, "in_proj").reshape(b, t, -1)
        lb = jnp.cumsum(jax.nn.softmax(hgrn_lb_logits.astype(F32), axis=0), axis=0)[layer]
        o_hg = hgrn2(proj, lb, hgrn_head_norm[layer], hg_heads)
        o_na = neighbourhood_attention(proj, na_rpb[layer], na_heads, 5 * hg_heads)
        mixed = matmul_concat(o_hg.reshape(b * t, -1), o_na.reshape(b * t, -1), w_out[layer].astype(BF16),
                              BF16, 1024, 512, "out_proj")
        h, u = resnorm(h, mixed, mix_norm_post[layer], 1.0, ffn2_norm_pre[layer])

        ff = _swiglu_ffn(u, ffn2_w_gate[layer], ffn2_w_up[layer], ffn2_w_down[layer])
        if layer + 1 < depth:
            h, u = resnorm(h, ff, ffn2_norm_post[layer], FFN_RES_WEIGHT, ffn1_norm_pre[layer + 1])
        else:
            h = resnorm(h, ff, ffn2_norm_post[layer], FFN_RES_WEIGHT)
    return h.reshape(b, t, d)
```

```python
import functools
import math

import numpy as np
import jax
import jax.numpy as jnp
from jax import lax
from jax.experimental import pallas as pl
from jax.experimental.pallas import tpu as pltpu

EPS = 1e-6
HEAD_DIM = 128
GRID_W = 64
NA_WIN_ROWS = 8
NA_WIN_COLS = 16
HG_CHUNK = 128
HG_TABLE_LEVELS = (1, 2)
HG_CHUNK_UNROLL = 8
NA_GROUP = 4
NA_SLAB_ROWS = NA_WIN_ROWS + NA_GROUP
NA_GROUP_UNROLL = 4
FFN_RES_WEIGHT = 0.5
NEG_BIG = -1e30
V7X_VMEM_LIMIT_BYTES = 56 * 1024 * 1024

BF16 = jnp.bfloat16
F32 = jnp.float32


def _params(*sem):
    return pltpu.CompilerParams(dimension_semantics=sem, vmem_limit_bytes=V7X_VMEM_LIMIT_BYTES)


def _dot(a, b):
    return jnp.dot(a, b, preferred_element_type=F32)


def _dot_nt(a, b):
    return lax.dot_general(a, b, (((1,), (1,)), ((), ())), preferred_element_type=F32)


def _dot_tn(a, b):
    return lax.dot_general(a, b, (((0,), (0,)), ((), ())), preferred_element_type=F32)


def _sigmoid(x):
    return 1.0 / (1.0 + jnp.exp(-x))


def _silu(x):
    return x * _sigmoid(x)


def _tile(n, pref, mult):
    t = (min(pref, n) // mult) * mult
    while t >= mult:
        if n % t == 0:
            return t
        t -= mult
    return n


def _rmsnorm_cast_kernel(x_ref, g_ref, o_ref):
    x = x_ref[...]
    inv = lax.rsqrt(jnp.mean(x * x, axis=-1, keepdims=True) + EPS)
    o_ref[...] = (x * inv * g_ref[...]).astype(o_ref.dtype)


def rmsnorm_cast(x, gain):
    m, d = x.shape
    tm = _tile(m, 256, 8)
    return pl.pallas_call(
        _rmsnorm_cast_kernel,
        grid=(m // tm,),
        in_specs=[pl.BlockSpec((tm, d), lambda i: (i, 0)), pl.BlockSpec((1, d), lambda i: (0, 0))],
        out_specs=pl.BlockSpec((tm, d), lambda i: (i, 0)),
        out_shape=jax.ShapeDtypeStruct((m, d), BF16),
        compiler_params=_params("parallel"),
        name="rmsnorm_cast",
    )(x, gain.reshape(1, d))


def _resnorm_kernel(h_ref, y_ref, gp_ref, gn_ref, hn_ref, u_ref, *, weight):
    y = y_ref[...].astype(F32)
    inv = lax.rsqrt(jnp.mean(y * y, axis=-1, keepdims=True) + EPS)
    h = h_ref[...] + weight * (y * inv * gp_ref[...])
    hn_ref[...] = h
    inv2 = lax.rsqrt(jnp.mean(h * h, axis=-1, keepdims=True) + EPS)
    u_ref[...] = (h * inv2 * gn_ref[...]).astype(u_ref.dtype)


def _resnorm_last_kernel(h_ref, y_ref, gp_ref, hn_ref, *, weight):
    y = y_ref[...].astype(F32)
    inv = lax.rsqrt(jnp.mean(y * y, axis=-1, keepdims=True) + EPS)
    hn_ref[...] = h_ref[...] + weight * (y * inv * gp_ref[...])


def resnorm(h, y, gain_post, weight, gain_next=None):
    m, d = h.shape
    tm = _tile(m, 256, 8)
    row = pl.BlockSpec((tm, d), lambda i: (i, 0))
    vec = pl.BlockSpec((1, d), lambda i: (0, 0))
    if gain_next is None:
        return pl.pallas_call(
            functools.partial(_resnorm_last_kernel, weight=weight),
            grid=(m // tm,),
            in_specs=[row, row, vec],
            out_specs=row,
            out_shape=jax.ShapeDtypeStruct((m, d), F32),
            compiler_params=_params("parallel"),
            name="resnorm_last",
        )(h, y, gain_post.reshape(1, d))
    return pl.pallas_call(
        functools.partial(_resnorm_kernel, weight=weight),
        grid=(m // tm,),
        in_specs=[row, row, vec, vec],
        out_specs=[row, row],
        out_shape=[jax.ShapeDtypeStruct((m, d), F32), jax.ShapeDtypeStruct((m, d), BF16)],
        compiler_params=_params("parallel"),
        name="resnorm",
    )(h, y, gain_post.reshape(1, d), gain_next.reshape(1, d))


def _gateup_kernel(x_ref, wg_ref, wu_ref, o_ref):
    x = x_ref[...]
    g = _dot(x, wg_ref[...].astype(BF16))
    u = _dot(x, wu_ref[...].astype(BF16))
    o_ref[...] = (_silu(g) * u).astype(o_ref.dtype)


def gateup(x, wg, wu):
    m, k = x.shape
    n = wg.shape[1]
    tm = _tile(m, 1024, 8)
    tn = _tile(n, 256, 128)
    return pl.pallas_call(
        _gateup_kernel,
        grid=(m // tm, n // tn),
        in_specs=[pl.BlockSpec((tm, k), lambda i, j: (i, 0)),
                  pl.BlockSpec((k, tn), lambda i, j: (0, j)),
                  pl.BlockSpec((k, tn), lambda i, j: (0, j))],
        out_specs=pl.BlockSpec((tm, tn), lambda i, j: (i, j)),
        out_shape=jax.ShapeDtypeStruct((m, n), BF16),
        compiler_params=_params("parallel", "arbitrary"),
        name="ffn_gateup",
    )(x, wg, wu)


def _mm_kernel(a_ref, b_ref, o_ref):
    o_ref[...] = _dot(a_ref[...], b_ref[...].astype(BF16)).astype(o_ref.dtype)


def matmul(a, b, out_dtype, tm_pref, tn_pref, name, rhs_resident=False):
    m, k = a.shape
    n = b.shape[1]
    tm = _tile(m, tm_pref, 8)
    tn = _tile(n, tn_pref, 128)
    once = pl.Buffered(1)
    if rhs_resident:
        grid = (n // tn, m // tm)
        a_spec = pl.BlockSpec((tm, k), lambda j, i: (i, 0))
        b_spec = pl.BlockSpec((k, tn), lambda j, i: (0, j), pipeline_mode=once)
        o_spec = pl.BlockSpec((tm, tn), lambda j, i: (i, j))
    else:
        grid = (m // tm, n // tn)
        a_spec = pl.BlockSpec((tm, k), lambda i, j: (i, 0), pipeline_mode=once)
        b_spec = pl.BlockSpec((k, tn), lambda i, j: (0, j))
        o_spec = pl.BlockSpec((tm, tn), lambda i, j: (i, j))
    return pl.pallas_call(
        _mm_kernel,
        grid=grid,
        in_specs=[a_spec, b_spec],
        out_specs=o_spec,
        out_shape=jax.ShapeDtypeStruct((m, n), out_dtype),
        compiler_params=_params("parallel", "arbitrary"),
        name=name,
    )(a, b)


def _mm2_kernel(a1_ref, a2_ref, b_ref, o_ref):
    k1 = a1_ref.shape[1]
    o_ref[...] = (_dot(a1_ref[...], b_ref[:k1, :]) + _dot(a2_ref[...], b_ref[k1:, :])).astype(o_ref.dtype)


def matmul_concat(a1, a2, b, out_dtype, tm_pref, tn_pref, name):
    m, k1 = a1.shape
    k2 = a2.shape[1]
    n = b.shape[1]
    tm = _tile(m, tm_pref, 8)
    tn = _tile(n, tn_pref, 128)
    return pl.pallas_call(
        _mm2_kernel,
        grid=(m // tm, n // tn),
        in_specs=[pl.BlockSpec((tm, k1), lambda i, j: (i, 0)),
                  pl.BlockSpec((tm, k2), lambda i, j: (i, 0)),
                  pl.BlockSpec((k1 + k2, tn), lambda i, j: (0, j))],
        out_specs=pl.BlockSpec((tm, tn), lambda i, j: (i, j)),
        out_shape=jax.ShapeDtypeStruct((m, n), out_dtype),
        compiler_params=_params("parallel", "arbitrary"),
        name=name,
    )(a1, a2, b)


def _hgrn_consts(chunk, reverse):
    nl = int(math.log2(chunk))
    assert 1 << nl == chunk
    t = np.arange(chunk)
    u = t[None, :]
    level_sums = np.zeros((nl, chunk, chunk), np.float32)
    qside = np.zeros((nl, chunk), np.float32)
    amask = np.zeros((nl + 1, chunk, chunk), np.float32)
    for l in range(nl):
        half = 1 << l
        blk = t // (2 * half)
        mid = blk * 2 * half + half
        upper = t >= mid
        level_sums[l] = np.where(upper[:, None],
                                 (u >= mid[:, None]) & (u <= t[:, None]),
                                 (u > t[:, None]) & (u <= mid[:, None] - 1))
        qside[l] = upper
        amask[l] = (blk[:, None] == blk[None, :]) & upper[:, None] & ~upper[None, :]
    amask[nl] = np.eye(chunk)
    sums = np.stack([level_sums[l] for l in HG_TABLE_LEVELS] + [(u <= t[:, None]).astype(np.float32)])
    if reverse:
        sums = sums[:, ::-1, ::-1]
        qside = qside[:, ::-1]
        amask = amask[:, ::-1, ::-1]
    sums = np.ascontiguousarray(sums).reshape(len(sums) * chunk, chunk)
    sums = np.concatenate([sums, sums], axis=1)
    qside = 2.0 * qside - 1.0
    qside = np.ascontiguousarray(np.broadcast_to(qside[:, :, None], (nl, chunk, HEAD_DIM)))
    return (jnp.asarray(sums, BF16), jnp.asarray(qside, F32),
            jnp.asarray(np.ascontiguousarray(amask), F32))


def _block_reference(cum, level, reverse):
    chunk = cum.shape[0]
    half = 1 << level
    pieces = []
    for start in range(0, chunk, 2 * half):
        row = start + half if reverse else start + half - 1
        pieces.append(jnp.broadcast_to(cum[row:row + 1, :], (2 * half, cum.shape[1])))
    return pieces[0] if len(pieces) == 1 else jnp.concatenate(pieces, axis=0)


def _hgrn_kernel(hq_ref, ff_ref, fb_ref, hi_ref, hg_ref, lb_ref, hn_ref, *rest, chunk, nl):
    consts_f, consts_b = rest[:3], rest[3:6]
    out_ref, of_ref, ob_ref, stf_ref, stb_ref = rest[6:]
    seq = hq_ref.shape[0]
    nchunks = seq // chunk
    stf_ref[...] = jnp.zeros_like(stf_ref)
    stb_ref[...] = jnp.zeros_like(stb_ref)

    def chunk_step(base, direction, f_ref, consts, st_ref, o_ref):
        sums_ref, side_ref, am_ref = consts
        rows = pl.ds(pl.multiple_of(base, chunk), chunk)
        q = _silu(hq_ref[rows, :])
        lb = lb_ref[direction:direction + 1, :]
        f = lb + (1.0 - lb) * _sigmoid(f_ref[rows, :])
        k = 1.0 - f
        g = jnp.log(f)
        v = hi_ref[rows, :].astype(BF16)
        g1 = g.astype(BF16)
        g2 = (g - g1.astype(F32)).astype(BF16)
        z = _dot(sums_ref[...], jnp.concatenate([g1, g2], axis=0))
        cum = z[len(HG_TABLE_LEVELS) * chunk:]
        a = am_ref[nl] * _dot_nt(q.astype(BF16), k.astype(BF16))
        for l in range(nl):
            side = side_ref[l]
            qside = side > 0.0
            if l == 0:
                x = jnp.where(qside, q * f, k)
            else:
                if l in HG_TABLE_LEVELS:
                    i = HG_TABLE_LEVELS.index(l)
                    zl = z[i * chunk:(i + 1) * chunk]
                else:
                    zl = (cum - _block_reference(cum, l, bool(direction))) * side
                x = jnp.where(qside, q, k) * jnp.exp(zl)
            a = a + am_ref[l] * _dot_nt(x.astype(BF16), x.astype(BF16))
        last = 0 if direction else chunk - 1
        cum_last = cum[last:last + 1, :]
        qd = (q * jnp.exp(cum)).astype(BF16)
        kd = (k * jnp.exp(cum_last - cum)).astype(BF16)
        st = st_ref[...]
        o_ref[rows, :] = _dot(a.astype(BF16), v) + _dot_nt(qd, st.astype(BF16))
        st_ref[...] = st * jnp.exp(cum_last) + _dot_tn(v, kd)

    def body(n, carry):
        chunk_step(n * chunk, 0, ff_ref, consts_f, stf_ref, of_ref)
        chunk_step((nchunks - 1 - n) * chunk, 1, fb_ref, consts_b, stb_ref, ob_ref)
        return carry

    lax.fori_loop(0, nchunks, body, 0, unroll=min(HG_CHUNK_UNROLL, nchunks))

    def finish(n, carry):
        rows = pl.ds(pl.multiple_of(n * chunk, chunk), chunk)
        o = of_ref[rows, :] + ob_ref[rows, :]
        inv = lax.rsqrt(jnp.mean(o * o, axis=-1, keepdims=True) + EPS)
        out_ref[rows, :] = (o * inv * hn_ref[...] * _silu(hg_ref[rows, :])).astype(out_ref.dtype)
        return carry

    lax.fori_loop(0, nchunks, finish, 0)


def hgrn2(proj, lb, head_norm, n_heads):
    b, t, _ = proj.shape
    chunk = min(HG_CHUNK, t)
    nl = int(math.log2(chunk))
    consts_f = _hgrn_consts(chunk, False)
    consts_b = _hgrn_consts(chunk, True)

    def col(group):
        return pl.BlockSpec((None, t, HEAD_DIM), lambda bi, hi, g=group: (bi, 0, g * n_heads + hi))

    def const(arr):
        return pl.BlockSpec(arr.shape, lambda bi, hi, nd=arr.ndim: (0,) * nd)

    return pl.pallas_call(
        functools.partial(_hgrn_kernel, chunk=chunk, nl=nl),
        grid=(b, n_heads),
        in_specs=[col(0), col(1), col(2), col(3), col(4),
                  pl.BlockSpec((2, HEAD_DIM), lambda bi, hi: (0, hi)),
                  pl.BlockSpec((1, HEAD_DIM), lambda bi, hi: (0, 0)),
                  *[const(c) for c in consts_f], *[const(c) for c in consts_b]],
        out_specs=pl.BlockSpec((None, t, HEAD_DIM), lambda bi, hi: (bi, 0, hi)),
        out_shape=jax.ShapeDtypeStruct((b, t, n_heads * HEAD_DIM), BF16),
        scratch_shapes=[pltpu.VMEM((t, HEAD_DIM), F32), pltpu.VMEM((t, HEAD_DIM), F32),
                        pltpu.VMEM((HEAD_DIM, HEAD_DIM), F32), pltpu.VMEM((HEAD_DIM, HEAD_DIM), F32)],
        compiler_params=_params("parallel", "parallel"),
        name="hgrn2",
    )(proj, proj, proj, proj, proj, lb, head_norm.reshape(1, HEAD_DIM), *consts_f, *consts_b)


def _na_col_mask():
    col = np.arange(GRID_W)
    col_start = np.clip(col - NA_WIN_COLS // 2, 0, GRID_W - NA_WIN_COLS)
    inside = (col[None, :] >= col_start[:, None]) & (col[None, :] < col_start[:, None] + NA_WIN_COLS)
    return jnp.asarray(np.tile(np.where(inside, 0.0, NEG_BIG), (1, 128 // GRID_W)), F32)


def _na_build_bias(rpb_ref, cmask_ref, tab_ref, bias_ref):
    lane = lax.broadcasted_iota(jnp.int32, (GRID_W, 128), 1)
    first_half = lane < GRID_W
    back = NA_WIN_COLS - 1
    n_dr = 2 * NA_WIN_ROWS - 1
    for dr in range(n_dr):
        row = jnp.broadcast_to(rpb_ref[dr:dr + 1, :], (GRID_W, 128))
        lo = pltpu.roll(row, 128 - back, 1, stride=1, stride_axis=0)
        hi = pltpu.roll(row, GRID_W - back, 1, stride=1, stride_axis=0)
        tab_ref[dr] = jnp.where(first_half, lo, hi) + cmask_ref[...]
    tab_ref[n_dr] = jnp.full((GRID_W, 128), NEG_BIG, F32)
    for variant, delta in enumerate((0, -NA_GROUP, -2 * NA_GROUP)):
        for i in range(NA_GROUP):
            win_lo = (0, i, NA_GROUP)[variant]

            def tile(j):
                inside = win_lo <= j < win_lo + NA_WIN_ROWS
                return tab_ref[delta + j - i + NA_WIN_ROWS - 1 if inside else n_dr]

            for pair in range(NA_SLAB_ROWS // 2):
                bias_ref[variant, i * GRID_W:(i + 1) * GRID_W, pair * 128:(pair + 1) * 128] = jnp.where(
                    first_half, tile(2 * pair), tile(2 * pair + 1))


def _na_kernel(q_ref, k_ref, v_ref, rpb_ref, cmask_ref, out_ref, kb_ref, vb_ref, tab_ref, bias_ref, *, scale):
    seq = q_ref.shape[0]
    n_groups = seq // GRID_W // NA_GROUP
    group = NA_GROUP * GRID_W
    slab = NA_SLAB_ROWS * GRID_W
    kb_ref[...] = k_ref[...].astype(BF16)
    vb_ref[...] = v_ref[...].astype(BF16)
    _na_build_bias(rpb_ref, cmask_ref, tab_ref, bias_ref)

    def body(gi, carry):
        slab_start = jnp.clip(gi - 1, 0, n_groups - NA_SLAB_ROWS // NA_GROUP) * group
        variant = jnp.where(gi == 0, 0, jnp.where(gi == n_groups - 1, 2, 1))
        qrows = pl.ds(pl.multiple_of(gi * group, group), group)
        krows = pl.ds(pl.multiple_of(slab_start, group), slab)
        q = (q_ref[qrows, :] * scale).astype(BF16)
        s = _dot_nt(q, kb_ref[krows, :]) + bias_ref[variant]
        m = jnp.max(s, axis=-1, keepdims=True)
        p = jnp.exp(s - m)
        denom = jnp.sum(p, axis=-1, keepdims=True)
        o = _dot(p.astype(BF16), vb_ref[krows, :]) / denom
        out_ref[qrows, :] = o.astype(out_ref.dtype)
        return carry

    lax.fori_loop(0, n_groups, body, 0, unroll=NA_GROUP_UNROLL)


def neighbourhood_attention(proj, rpb, n_heads, col0):
    b, t, _ = proj.shape
    n_rows = t // GRID_W
    assert NA_GROUP == NA_WIN_ROWS // 2 and NA_SLAB_ROWS % NA_GROUP == 0 and (NA_SLAB_ROWS * GRID_W) % 128 == 0
    assert t % GRID_W == 0 and n_rows % NA_GROUP == 0 and n_rows >= NA_SLAB_ROWS + NA_GROUP
    n_dr, n_dc = rpb.shape[1:]
    assert (n_dr, n_dc) == (2 * NA_WIN_ROWS - 1, 2 * NA_WIN_COLS - 1)
    rpb_pad = jnp.pad(rpb.astype(F32), ((0, 0), (0, 1), (0, 128 - n_dc)))
    cmask = _na_col_mask()

    def col(group):
        return pl.BlockSpec((None, t, HEAD_DIM), lambda bi, hi, g=group: (bi, 0, col0 + g * n_heads + hi))

    return pl.pallas_call(
        functools.partial(_na_kernel, scale=HEAD_DIM ** -0.5),
        grid=(b, n_heads),
        in_specs=[col(0), col(1), col(2),
                  pl.BlockSpec((None, n_dr + 1, 128), lambda bi, hi: (hi, 0, 0)),
                  pl.BlockSpec(cmask.shape, lambda bi, hi: (0, 0))],
        out_specs=pl.BlockSpec((None, t, HEAD_DIM), lambda bi, hi: (bi, 0, hi)),
        out_shape=jax.ShapeDtypeStruct((b, t, n_heads * HEAD_DIM), BF16),
        scratch_shapes=[pltpu.VMEM((t, HEAD_DIM), BF16), pltpu.VMEM((t, HEAD_DIM), BF16),
                        pltpu.VMEM((n_dr + 1, GRID_W, 128), F32),
                        pltpu.VMEM((3, NA_GROUP * GRID_W, NA_SLAB_ROWS * GRID_W), F32)],
        compiler_params=_params("parallel", "parallel"),
        name="neighbourhood_attention",
    )(proj, proj, proj, rpb_pad, cmask)


def _swiglu_ffn(u, w_gate, w_up, w_down):
    act = gateup(u, w_gate, w_up)
    return matmul(act, w_down.astype(BF16), BF16, 512, 1024, "ffn_down", rhs_resident=True)


def kernel(x, ffn1_norm_pre, ffn1_w_gate, ffn1_w_up, ffn1_w_down, ffn1_norm_post, mix_norm_pre, w_in, hgrn_lb_logits, hgrn_head_norm, na_rpb, w_out, mix_norm_post, ffn2_norm_pre, ffn2_w_gate, ffn2_w_up, ffn2_w_down, ffn2_norm_post):
    b, t, d = x.shape
    depth = w_in.shape[0]
    hg_width = hgrn_lb_logits.shape[-1]
    hg_heads = hg_width // HEAD_DIM
    na_heads = na_rpb.shape[1]
    h = x.reshape(b * t, d)
    u = rmsnorm_cast(h, ffn1_norm_pre[0])
    for layer in range(depth):
        ff = _swiglu_ffn(u, ffn1_w_gate[layer], ffn1_w_up[layer], ffn1_w_down[layer])
        h, u = resnorm(h, ff, ffn1_norm_post[layer], FFN_RES_WEIGHT, mix_norm_pre[layer])

        proj = matmul(u, w_in[layer], F32, 2048, 512, "in_proj").reshape(b, t, -1)
        lb = jnp.cumsum(jax.nn.softmax(hgrn_lb_logits.astype(F32), axis=0), axis=0)[layer]
        o_hg = hgrn2(proj, lb, hgrn_head_norm[layer], hg_heads)
        o_na = neighbourhood_attention(proj, na_rpb[layer], na_heads, 5 * hg_heads)
        mixed = matmul_concat(o_hg.reshape(b * t, -1), o_na.reshape(b * t, -1), w_out[layer].astype(BF16),
                              BF16, 1024, 512, "out_proj")
        h, u = resnorm(h, mixed, mix_norm_post[layer], 1.0, ffn2_norm_pre[layer])

        ff = _swiglu_ffn(u, ffn2_w_gate[layer], ffn2_w_up[layer], ffn2_w_down[layer])
        if layer + 1 < depth:
            h, u = resnorm(h, ff, ffn2_norm_post[layer], FFN_RES_WEIGHT, ffn1_norm_pre[layer + 1])
        else:
            h = resnorm(h, ff, ffn2_norm_post[layer], FFN_RES_WEIGHT)
    return h.reshape(b, t, d)
```

```python
import functools
import math

import numpy as np
import jax
import jax.numpy as jnp
from jax import lax
from jax.experimental import pallas as pl
from jax.experimental.pallas import tpu as pltpu

EPS = 1e-6
HEAD_DIM = 128
GRID_W = 64
NA_WIN_ROWS = 8
NA_WIN_COLS = 16
HG_CHUNK = 128
HG_TABLE_LEVELS = (1, 2)
HG_CHUNK_UNROLL = 8
NA_GROUP = 4
NA_SLAB_ROWS = NA_WIN_ROWS + NA_GROUP
NA_GROUP_UNROLL = 4
FFN_RES_WEIGHT = 0.5
NEG_BIG = -1e30
V7X_VMEM_LIMIT_BYTES = 56 * 1024 * 1024

BF16 = jnp.bfloat16
F32 = jnp.float32


def _params(*sem):
    return pltpu.CompilerParams(dimension_semantics=sem, vmem_limit_bytes=V7X_VMEM_LIMIT_BYTES)


def _dot(a, b):
    return jnp.dot(a, b, preferred_element_type=F32)


def _dot_nt(a, b):
    return lax.dot_general(a, b, (((1,), (1,)), ((), ())), preferred_element_type=F32)


def _dot_tn(a, b):
    return lax.dot_general(a, b, (((0,), (0,)), ((), ())), preferred_element_type=F32)


def _sigmoid(x):
    return 1.0 / (1.0 + jnp.exp(-x))


def _silu(x):
    return x * _sigmoid(x)


def _tile(n, pref, mult):
    t = (min(pref, n) // mult) * mult
    while t >= mult:
        if n % t == 0:
            return t
        t -= mult
    return n


def _rmsnorm_cast_kernel(x_ref, g_ref, o_ref):
    x = x_ref[...]
    inv = lax.rsqrt(jnp.mean(x * x, axis=-1, keepdims=True) + EPS)
    o_ref[...] = (x * inv * g_ref[...]).astype(o_ref.dtype)


def rmsnorm_cast(x, gain):
    m, d = x.shape
    tm = _tile(m, 256, 8)
    return pl.pallas_call(
        _rmsnorm_cast_kernel,
        grid=(m // tm,),
        in_specs=[pl.BlockSpec((tm, d), lambda i: (i, 0)), pl.BlockSpec((1, d), lambda i: (0, 0))],
        out_specs=pl.BlockSpec((tm, d), lambda i: (i, 0)),
        out_shape=jax.ShapeDtypeStruct((m, d), BF16),
        compiler_params=_params("parallel"),
        name="rmsnorm_cast",
    )(x, gain.reshape(1, d))


def _resnorm_kernel(h_ref, y_ref, gp_ref, gn_ref, hn_ref, u_ref, *, weight):
    y = y_ref[...].astype(F32)
    inv = lax.rsqrt(jnp.mean(y * y, axis=-1, keepdims=True) + EPS)
    h = h_ref[...] + weight * (y * inv * gp_ref[...])
    hn_ref[...] = h
    inv2 = lax.rsqrt(jnp.mean(h * h, axis=-1, keepdims=True) + EPS)
    u_ref[...] = (h * inv2 * gn_ref[...]).astype(u_ref.dtype)


def _resnorm_last_kernel(h_ref, y_ref, gp_ref, hn_ref, *, weight):
    y = y_ref[...].astype(F32)
    inv = lax.rsqrt(jnp.mean(y * y, axis=-1, keepdims=True) + EPS)
    hn_ref[...] = h_ref[...] + weight * (y * inv * gp_ref[...])


def resnorm(h, y, gain_post, weight, gain_next=None):
    m, d = h.shape
    tm = _tile(m, 256, 8)
    row = pl.BlockSpec((tm, d), lambda i: (i, 0))
    vec = pl.BlockSpec((1, d), lambda i: (0, 0))
    if gain_next is None:
        return pl.pallas_call(
            functools.partial(_resnorm_last_kernel, weight=weight),
            grid=(m // tm,),
            in_specs=[row, row, vec],
            out_specs=row,
            out_shape=jax.ShapeDtypeStruct((m, d), F32),
            compiler_params=_params("parallel"),
            name="resnorm_last",
        )(h, y, gain_post.reshape(1, d))
    return pl.pallas_call(
        functools.partial(_resnorm_kernel, weight=weight),
        grid=(m // tm,),
        in_specs=[row, row, vec, vec],
        out_specs=[row, row],
        out_shape=[jax.ShapeDtypeStruct((m, d), F32), jax.ShapeDtypeStruct((m, d), BF16)],
        compiler_params=_params("parallel"),
        name="resnorm",
    )(h, y, gain_post.reshape(1, d), gain_next.reshape(1, d))


def _gateup_kernel(x_ref, wg_ref, wu_ref, o_ref):
    x = x_ref[...]
    g = _dot(x, wg_ref[...].astype(BF16))
    u = _dot(x, wu_ref[...].astype(BF16))
    o_ref[...] = (_silu(g) * u).astype(o_ref.dtype)


def gateup(x, wg, wu):
    m, k = x.shape
    n = wg.shape[1]
    tm = _tile(m, 1024, 8)
    tn = _tile(n, 256, 128)
    return pl.pallas_call(
        _gateup_kernel,
        grid=(m // tm, n // tn),
        in_specs=[pl.BlockSpec((tm, k), lambda i, j: (i, 0)),
                  pl.BlockSpec((k, tn), lambda i, j: (0, j)),
                  pl.BlockSpec((k, tn), lambda i, j: (0, j))],
        out_specs=pl.BlockSpec((tm, tn), lambda i, j: (i, j)),
        out_shape=jax.ShapeDtypeStruct((m, n), BF16),
        compiler_params=_params("parallel", "arbitrary"),
        name="ffn_gateup",
    )(x, wg, wu)


def _mm_kernel(a_ref, b_ref, o_ref):
    o_ref[...] = _dot(a_ref[...], b_ref[...].astype(BF16)).astype(o_ref.dtype)


def matmul(a, b, out_dtype, tm_pref, tn_pref, name, rhs_resident=False):
    m, k = a.shape
    n = b.shape[1]
    tm = _tile(m, tm_pref, 8)
    tn = _tile(n, tn_pref, 128)
    once = pl.Buffered(1)
    if rhs_resident:
        grid = (n // tn, m // tm)
        a_spec = pl.BlockSpec((tm, k), lambda j, i: (i, 0))
        b_spec = pl.BlockSpec((k, tn), lambda j, i: (0, j), pipeline_mode=once)
        o_spec = pl.BlockSpec((tm, tn), lambda j, i: (i, j))
    else:
        grid = (m // tm, n // tn)
        a_spec = pl.BlockSpec((tm, k), lambda i, j: (i, 0), pipeline_mode=once)
        b_spec = pl.BlockSpec((k, tn), lambda i, j: (0, j))
        o_spec = pl.BlockSpec((tm, tn), lambda i, j: (i, j))
    return pl.pallas_call(
        _mm_kernel,
        grid=grid,
        in_specs=[a_spec, b_spec],
        out_specs=o_spec,
        out_shape=jax.ShapeDtypeStruct((m, n), out_dtype),
        compiler_params=_params("parallel", "arbitrary"),
        name=name,
    )(a, b)


def _mm2_kernel(a1_ref, a2_ref, b_ref, o_ref):
    k1 = a1_ref.shape[1]
    o_ref[...] = (_dot(a1_ref[...], b_ref[:k1, :]) + _dot(a2_ref[...], b_ref[k1:, :])).astype(o_ref.dtype)


def matmul_concat(a1, a2, b, out_dtype, tm_pref, tn_pref, name):
    m, k1 = a1.shape
    k2 = a2.shape[1]
    n = b.shape[1]
    tm = _tile(m, tm_pref, 8)
    tn = _tile(n, tn_pref, 128)
    return pl.pallas_call(
        _mm2_kernel,
        grid=(m // tm, n // tn),
        in_specs=[pl.BlockSpec((tm, k1), lambda i, j: (i, 0)),
                  pl.BlockSpec((tm, k2), lambda i, j: (i, 0)),
                  pl.BlockSpec((k1 + k2, tn), lambda i, j: (0, j))],
        out_specs=pl.BlockSpec((tm, tn), lambda i, j: (i, j)),
        out_shape=jax.ShapeDtypeStruct((m, n), out_dtype),
        compiler_params=_params("parallel", "arbitrary"),
        name=name,
    )(a1, a2, b)


def _hgrn_consts(chunk, reverse):
    nl = int(math.log2(chunk))
    assert 1 << nl == chunk
    t = np.arange(chunk)
    u = t[None, :]
    level_sums = np.zeros((nl, chunk, chunk), np.float32)
    qside = np.zeros((nl, chunk), np.float32)
    amask = np.zeros((nl + 1, chunk, chunk), np.float32)
    for l in range(nl):
        half = 1 << l
        blk = t // (2 * half)
        mid = blk * 2 * half + half
        upper = t >= mid
        level_sums[l] = np.where(upper[:, None],
                                 (u >= mid[:, None]) & (u <= t[:, None]),
                                 (u > t[:, None]) & (u <= mid[:, None] - 1))
        qside[l] = upper
        amask[l] = (blk[:, None] == blk[None, :]) & upper[:, None] & ~upper[None, :]
    amask[nl] = np.eye(chunk)
    sums = np.stack([level_sums[l] for l in HG_TABLE_LEVELS] + [(u <= t[:, None]).astype(np.float32)])
    if reverse:
        sums = sums[:, ::-1, ::-1]
        qside = qside[:, ::-1]
        amask = amask[:, ::-1, ::-1]
    sums = np.ascontiguousarray(sums).reshape(len(sums) * chunk, chunk)
    sums = np.concatenate([sums, sums], axis=1)
    qside = 2.0 * qside - 1.0
    qside = np.ascontiguousarray(np.broadcast_to(qside[:, :, None], (nl, chunk, HEAD_DIM)))
    return (jnp.asarray(sums, BF16), jnp.asarray(qside, F32),
            jnp.asarray(np.ascontiguousarray(amask), F32))


def _block_reference(cum, level, reverse):
    chunk = cum.shape[0]
    half = 1 << level
    pieces = []
    for start in range(0, chunk, 2 * half):
        row = start + half if reverse else start + half - 1
        pieces.append(jnp.broadcast_to(cum[row:row + 1, :], (2 * half, cum.shape[1])))
    return pieces[0] if len(pieces) == 1 else jnp.concatenate(pieces, axis=0)


def _hgrn_kernel(hq_ref, ff_ref, fb_ref, hi_ref, hg_ref, lb_ref, hn_ref, *rest, chunk, nl):
    consts_f, consts_b = rest[:3], rest[3:6]
    out_ref, of_ref, ob_ref, stf_ref, stb_ref = rest[6:]
    seq = hq_ref.shape[0]
    nchunks = seq // chunk
    stf_ref[...] = jnp.zeros_like(stf_ref)
    stb_ref[...] = jnp.zeros_like(stb_ref)

    def chunk_step(base, direction, f_ref, consts, st_ref, o_ref):
        sums_ref, side_ref, am_ref = consts
        rows = pl.ds(pl.multiple_of(base, chunk), chunk)
        q = _silu(hq_ref[rows, :])
        lb = lb_ref[direction:direction + 1, :]
        f = lb + (1.0 - lb) * _sigmoid(f_ref[rows, :])
        k = 1.0 - f
        g = jnp.log(f)
        v = hi_ref[rows, :].astype(BF16)
        g1 = g.astype(BF16)
        g2 = (g - g1.astype(F32)).astype(BF16)
        z = _dot(sums_ref[...], jnp.concatenate([g1, g2], axis=0))
        cum = z[len(HG_TABLE_LEVELS) * chunk:]
        a = am_ref[nl] * _dot_nt(q.astype(BF16), k.astype(BF16))
        for l in range(nl):
            side = side_ref[l]
            qside = side > 0.0
            if l == 0:
                x = jnp.where(qside, q * f, k)
            else:
                if l in HG_TABLE_LEVELS:
                    i = HG_TABLE_LEVELS.index(l)
                    zl = z[i * chunk:(i + 1) * chunk]
                else:
                    zl = (cum - _block_reference(cum, l, bool(direction))) * side
                x = jnp.where(qside, q, k) * jnp.exp(zl)
            a = a + am_ref[l] * _dot_nt(x.astype(BF16), x.astype(BF16))
        last = 0 if direction else chunk - 1
        cum_last = cum[last:last + 1, :]
        qd = (q * jnp.exp(cum)).astype(BF16)
        kd = (k * jnp.exp(cum_last - cum)).astype(BF16)
        st = st_ref[...]
        o_ref[rows, :] = _dot(a.astype(BF16), v) + _dot_nt(qd, st.astype(BF16))
        st_ref[...] = st * jnp.exp(cum_last) + _dot_tn(v, kd)

    def finish(base):
        rows = pl.ds(pl.multiple_of(base, chunk), chunk)
        o = of_ref[rows, :] + ob_ref[rows, :]
        inv = lax.rsqrt(jnp.mean(o * o, axis=-1, keepdims=True) + EPS)
        out_ref[rows, :] = (o * inv * hn_ref[...] * _silu(hg_ref[rows, :])).astype(out_ref.dtype)

    def body(n, carry, both_done):
        fwd_base, bwd_base = n * chunk, (nchunks - 1 - n) * chunk
        chunk_step(fwd_base, 0, ff_ref, consts_f, stf_ref, of_ref)
        chunk_step(bwd_base, 1, fb_ref, consts_b, stb_ref, ob_ref)
        if both_done:
            finish(fwd_base)
            finish(bwd_base)
        return carry

    crossed = nchunks // 2
    lax.fori_loop(0, crossed, functools.partial(body, both_done=False), 0,
                  unroll=max(1, min(HG_CHUNK_UNROLL, crossed)))
    lax.fori_loop(crossed, nchunks, functools.partial(body, both_done=True), 0,
                  unroll=min(HG_CHUNK_UNROLL, nchunks - crossed))


def hgrn2(proj, lb, head_norm, n_heads):
    b, t, _ = proj.shape
    chunk = min(HG_CHUNK, t)
    nl = int(math.log2(chunk))
    consts_f = _hgrn_consts(chunk, False)
    consts_b = _hgrn_consts(chunk, True)

    def col(group):
        return pl.BlockSpec((None, t, HEAD_DIM), lambda bi, hi, g=group: (bi, 0, g * n_heads + hi))

    def const(arr):
        return pl.BlockSpec(arr.shape, lambda bi, hi, nd=arr.ndim: (0,) * nd)

    return pl.pallas_call(
        functools.partial(_hgrn_kernel, chunk=chunk, nl=nl),
        grid=(b, n_heads),
        in_specs=[col(0), col(1), col(2), col(3), col(4),
                  pl.BlockSpec((2, HEAD_DIM), lambda bi, hi: (0, hi)),
                  pl.BlockSpec((1, HEAD_DIM), lambda bi, hi: (0, 0)),
                  *[const(c) for c in consts_f], *[const(c) for c in consts_b]],
        out_specs=pl.BlockSpec((None, t, HEAD_DIM), lambda bi, hi: (bi, 0, hi)),
        out_shape=jax.ShapeDtypeStruct((b, t, n_heads * HEAD_DIM), BF16),
        scratch_shapes=[pltpu.VMEM((t, HEAD_DIM), F32), pltpu.VMEM((t, HEAD_DIM), F32),
                        pltpu.VMEM((HEAD_DIM, HEAD_DIM), F32), pltpu.VMEM((HEAD_DIM, HEAD_DIM), F32)],
        compiler_params=_params("parallel", "parallel"),
        name="hgrn2",
    )(proj, proj, proj, proj, proj, lb, head_norm.reshape(1, HEAD_DIM), *consts_f, *consts_b)


def _na_col_mask():
    col = np.arange(GRID_W)
    col_start = np.clip(col - NA_WIN_COLS // 2, 0, GRID_W - NA_WIN_COLS)
    inside = (col[None, :] >= col_start[:, None]) & (col[None, :] < col_start[:, None] + NA_WIN_COLS)
    return jnp.asarray(np.tile(np.where(inside, 0.0, NEG_BIG), (1, 128 // GRID_W)), F32)


def _na_build_bias(rpb_ref, cmask_ref, tab_ref, bias_ref):
    lane = lax.broadcasted_iota(jnp.int32, (GRID_W, 128), 1)
    first_half = lane < GRID_W
    back = NA_WIN_COLS - 1
    n_dr = 2 * NA_WIN_ROWS - 1
    for dr in range(n_dr):
        row = jnp.broadcast_to(rpb_ref[dr:dr + 1, :], (GRID_W, 128))
        lo = pltpu.roll(row, 128 - back, 1, stride=1, stride_axis=0)
        hi = pltpu.roll(row, GRID_W - back, 1, stride=1, stride_axis=0)
        tab_ref[dr] = jnp.where(first_half, lo, hi) + cmask_ref[...]
    tab_ref[n_dr] = jnp.full((GRID_W, 128), NEG_BIG, F32)
    for variant, delta in enumerate((0, -NA_GROUP, -2 * NA_GROUP)):
        for i in range(NA_GROUP):
            win_lo = (0, i, NA_GROUP)[variant]

            def tile(j):
                inside = win_lo <= j < win_lo + NA_WIN_ROWS
                return tab_ref[delta + j - i + NA_WIN_ROWS - 1 if inside else n_dr]

            for pair in range(NA_SLAB_ROWS // 2):
                bias_ref[variant, i * GRID_W:(i + 1) * GRID_W, pair * 128:(pair + 1) * 128] = jnp.where(
                    first_half, tile(2 * pair), tile(2 * pair + 1))


def _na_kernel(q_ref, k_ref, v_ref, rpb_ref, cmask_ref, out_ref, kb_ref, vb_ref, tab_ref, bias_ref, *, scale):
    seq = q_ref.shape[0]
    n_groups = seq // GRID_W // NA_GROUP
    group = NA_GROUP * GRID_W
    slab = NA_SLAB_ROWS * GRID_W
    kb_ref[...] = k_ref[...].astype(BF16)
    vb_ref[...] = v_ref[...].astype(BF16)
    _na_build_bias(rpb_ref, cmask_ref, tab_ref, bias_ref)

    def body(gi, carry):
        slab_start = jnp.clip(gi - 1, 0, n_groups - NA_SLAB_ROWS // NA_GROUP) * group
        variant = jnp.where(gi == 0, 0, jnp.where(gi == n_groups - 1, 2, 1))
        qrows = pl.ds(pl.multiple_of(gi * group, group), group)
        krows = pl.ds(pl.multiple_of(slab_start, group), slab)
        q = (q_ref[qrows, :] * scale).astype(BF16)
        s = _dot_nt(q, kb_ref[krows, :]) + bias_ref[variant]
        m = jnp.max(s, axis=-1, keepdims=True)
        p = jnp.exp(s - m)
        denom = jnp.sum(p, axis=-1, keepdims=True)
        o = _dot(p.astype(BF16), vb_ref[krows, :]) / denom
        out_ref[qrows, :] = o.astype(out_ref.dtype)
        return carry

    lax.fori_loop(0, n_groups, body, 0, unroll=NA_GROUP_UNROLL)


def neighbourhood_attention(proj, rpb, n_heads, col0):
    b, t, _ = proj.shape
    n_rows = t // GRID_W
    assert NA_GROUP == NA_WIN_ROWS // 2 and NA_SLAB_ROWS % NA_GROUP == 0 and (NA_SLAB_ROWS * GRID_W) % 128 == 0
    assert t % GRID_W == 0 and n_rows % NA_GROUP == 0 and n_rows >= NA_SLAB_ROWS + NA_GROUP
    n_dr, n_dc = rpb.shape[1:]
    assert (n_dr, n_dc) == (2 * NA_WIN_ROWS - 1, 2 * NA_WIN_COLS - 1)
    rpb_pad = jnp.pad(rpb.astype(F32), ((0, 0), (0, 1), (0, 128 - n_dc)))
    cmask = _na_col_mask()

    def col(group):
        return pl.BlockSpec((None, t, HEAD_DIM), lambda bi, hi, g=group: (bi, 0, col0 + g * n_heads + hi))

    return pl.pallas_call(
        functools.partial(_na_kernel, scale=HEAD_DIM ** -0.5),
        grid=(b, n_heads),
        in_specs=[col(0), col(1), col(2),
                  pl.BlockSpec((None, n_dr + 1, 128), lambda bi, hi: (hi, 0, 0)),
                  pl.BlockSpec(cmask.shape, lambda bi, hi: (0, 0))],
        out_specs=pl.BlockSpec((None, t, HEAD_DIM), lambda bi, hi: (bi, 0, hi)),
        out_shape=jax.ShapeDtypeStruct((b, t, n_heads * HEAD_DIM), BF16),
        scratch_shapes=[pltpu.VMEM((t, HEAD_DIM), BF16), pltpu.VMEM((t, HEAD_DIM), BF16),
                        pltpu.VMEM((n_dr + 1, GRID_W, 128), F32),
                        pltpu.VMEM((3, NA_GROUP * GRID_W, NA_SLAB_ROWS * GRID_W), F32)],
        compiler_params=_params("parallel", "parallel"),
        name="neighbourhood_attention",
    )(proj, proj, proj, rpb_pad, cmask)


def _swiglu_ffn(u, w_gate, w_up, w_down):
    act = gateup(u, w_gate, w_up)
    return matmul(act, w_down.astype(BF16), BF16, 512, 1024, "ffn_down", rhs_resident=True)


def kernel(x, ffn1_norm_pre, ffn1_w_gate, ffn1_w_up, ffn1_w_down, ffn1_norm_post, mix_norm_pre, w_in, hgrn_lb_logits, hgrn_head_norm, na_rpb, w_out, mix_norm_post, ffn2_norm_pre, ffn2_w_gate, ffn2_w_up, ffn2_w_down, ffn2_norm_post):
    b, t, d = x.shape
    depth = w_in.shape[0]
    hg_width = hgrn_lb_logits.shape[-1]
    hg_heads = hg_width // HEAD_DIM
    na_heads = na_rpb.shape[1]
    h = x.reshape(b * t, d)
    u = rmsnorm_cast(h, ffn1_norm_pre[0])
    for layer in range(depth):
        ff = _swiglu_ffn(u, ffn1_w_gate[layer], ffn1_w_up[layer], ffn1_w_down[layer])
        h, u = resnorm(h, ff, ffn1_norm_post[layer], FFN_RES_WEIGHT, mix_norm_pre[layer])

        proj = matmul(u, w_in[layer], F32, 2048, 512, "in_proj").reshape(b, t, -1)
        lb = jnp.cumsum(jax.nn.softmax(hgrn_lb_logits.astype(F32), axis=0), axis=0)[layer]
        o_hg = hgrn2(proj, lb, hgrn_head_norm[layer], hg_heads)
        o_na = neighbourhood_attention(proj, na_rpb[layer], na_heads, 5 * hg_heads)
        mixed = matmul_concat(o_hg.reshape(b * t, -1), o_na.reshape(b * t, -1), w_out[layer].astype(BF16),
                              BF16, 1024, 512, "out_proj")
        h, u = resnorm(h, mixed, mix_norm_post[layer], 1.0, ffn2_norm_pre[layer])

        ff = _swiglu_ffn(u, ffn2_w_gate[layer], ffn2_w_up[layer], ffn2_w_down[layer])
        if layer + 1 < depth:
            h, u = resnorm(h, ff, ffn2_norm_post[layer], FFN_RES_WEIGHT, ffn1_norm_pre[layer + 1])
        else:
            h = resnorm(h, ff, ffn2_norm_post[layer], FFN_RES_WEIGHT)
    return h.reshape(b, t, d)
```

```python
import functools
import math

import numpy as np
import jax
import jax.numpy as jnp
from jax import lax
from jax.experimental import pallas as pl
from jax.experimental.pallas import tpu as pltpu

EPS = 1e-6
HEAD_DIM = 128
GRID_W = 64
NA_WIN_ROWS = 8
NA_WIN_COLS = 16
HG_CHUNK = 128
HG_TABLE_LEVELS = (1, 2)
HG_CHUNK_UNROLL = 8
NA_GROUP = 4
NA_SLAB_ROWS = NA_WIN_ROWS + NA_GROUP
NA_GROUP_UNROLL = 4
NA_INTERLEAVE = 4
FFN_RES_WEIGHT = 0.5
NEG_BIG = -1e30
V7X_VMEM_LIMIT_BYTES = 56 * 1024 * 1024

BF16 = jnp.bfloat16
F32 = jnp.float32


def _params(*sem):
    return pltpu.CompilerParams(dimension_semantics=sem, vmem_limit_bytes=V7X_VMEM_LIMIT_BYTES)


def _dot(a, b):
    return jnp.dot(a, b, preferred_element_type=F32)


def _dot_nt(a, b):
    return lax.dot_general(a, b, (((1,), (1,)), ((), ())), preferred_element_type=F32)


def _dot_tn(a, b):
    return lax.dot_general(a, b, (((0,), (0,)), ((), ())), preferred_element_type=F32)


def _sigmoid(x):
    return 1.0 / (1.0 + jnp.exp(-x))


def _silu(x):
    return x * _sigmoid(x)


def _tile(n, pref, mult):
    t = (min(pref, n) // mult) * mult
    while t >= mult:
        if n % t == 0:
            return t
        t -= mult
    return n


def _rmsnorm_cast_kernel(x_ref, g_ref, o_ref):
    x = x_ref[...]
    inv = lax.rsqrt(jnp.mean(x * x, axis=-1, keepdims=True) + EPS)
    o_ref[...] = (x * inv * g_ref[...]).astype(o_ref.dtype)


def rmsnorm_cast(x, gain):
    m, d = x.shape
    tm = _tile(m, 256, 8)
    return pl.pallas_call(
        _rmsnorm_cast_kernel,
        grid=(m // tm,),
        in_specs=[pl.BlockSpec((tm, d), lambda i: (i, 0)), pl.BlockSpec((1, d), lambda i: (0, 0))],
        out_specs=pl.BlockSpec((tm, d), lambda i: (i, 0)),
        out_shape=jax.ShapeDtypeStruct((m, d), BF16),
        compiler_params=_params("parallel"),
        name="rmsnorm_cast",
    )(x, gain.reshape(1, d))


def _resnorm_kernel(h_ref, y_ref, gp_ref, gn_ref, hn_ref, u_ref, *, weight):
    y = y_ref[...].astype(F32)
    inv = lax.rsqrt(jnp.mean(y * y, axis=-1, keepdims=True) + EPS)
    h = h_ref[...] + weight * (y * inv * gp_ref[...])
    hn_ref[...] = h
    inv2 = lax.rsqrt(jnp.mean(h * h, axis=-1, keepdims=True) + EPS)
    u_ref[...] = (h * inv2 * gn_ref[...]).astype(u_ref.dtype)


def _resnorm_last_kernel(h_ref, y_ref, gp_ref, hn_ref, *, weight):
    y = y_ref[...].astype(F32)
    inv = lax.rsqrt(jnp.mean(y * y, axis=-1, keepdims=True) + EPS)
    hn_ref[...] = h_ref[...] + weight * (y * inv * gp_ref[...])


def resnorm(h, y, gain_post, weight, gain_next=None):
    m, d = h.shape
    tm = _tile(m, 256, 8)
    row = pl.BlockSpec((tm, d), lambda i: (i, 0))
    vec = pl.BlockSpec((1, d), lambda i: (0, 0))
    if gain_next is None:
        return pl.pallas_call(
            functools.partial(_resnorm_last_kernel, weight=weight),
            grid=(m // tm,),
            in_specs=[row, row, vec],
            out_specs=row,
            out_shape=jax.ShapeDtypeStruct((m, d), F32),
            compiler_params=_params("parallel"),
            name="resnorm_last",
        )(h, y, gain_post.reshape(1, d))
    return pl.pallas_call(
        functools.partial(_resnorm_kernel, weight=weight),
        grid=(m // tm,),
        in_specs=[row, row, vec, vec],
        out_specs=[row, row],
        out_shape=[jax.ShapeDtypeStruct((m, d), F32), jax.ShapeDtypeStruct((m, d), BF16)],
        compiler_params=_params("parallel"),
        name="resnorm",
    )(h, y, gain_post.reshape(1, d), gain_next.reshape(1, d))


def _gateup_kernel(x_ref, wg_ref, wu_ref, o_ref):
    x = x_ref[...]
    g = _dot(x, wg_ref[...].astype(BF16))
    u = _dot(x, wu_ref[...].astype(BF16))
    o_ref[...] = (_silu(g) * u).astype(o_ref.dtype)


def gateup(x, wg, wu):
    m, k = x.shape
    n = wg.shape[1]
    tm = _tile(m, 1024, 8)
    tn = _tile(n, 256, 128)
    return pl.pallas_call(
        _gateup_kernel,
        grid=(m // tm, n // tn),
        in_specs=[pl.BlockSpec((tm, k), lambda i, j: (i, 0)),
                  pl.BlockSpec((k, tn), lambda i, j: (0, j)),
                  pl.BlockSpec((k, tn), lambda i, j: (0, j))],
        out_specs=pl.BlockSpec((tm, tn), lambda i, j: (i, j)),
        out_shape=jax.ShapeDtypeStruct((m, n), BF16),
        compiler_params=_params("parallel", "arbitrary"),
        name="ffn_gateup",
    )(x, wg, wu)


def _mm_kernel(a_ref, b_ref, o_ref):
    o_ref[...] = _dot(a_ref[...], b_ref[...].astype(BF16)).astype(o_ref.dtype)


def matmul(a, b, out_dtype, tm_pref, tn_pref, name, rhs_resident=False):
    m, k = a.shape
    n = b.shape[1]
    tm = _tile(m, tm_pref, 8)
    tn = _tile(n, tn_pref, 128)
    once = pl.Buffered(1)
    if rhs_resident:
        grid = (n // tn, m // tm)
        a_spec = pl.BlockSpec((tm, k), lambda j, i: (i, 0))
        b_spec = pl.BlockSpec((k, tn), lambda j, i: (0, j), pipeline_mode=once)
        o_spec = pl.BlockSpec((tm, tn), lambda j, i: (i, j))
    else:
        grid = (m // tm, n // tn)
        a_spec = pl.BlockSpec((tm, k), lambda i, j: (i, 0), pipeline_mode=once)
        b_spec = pl.BlockSpec((k, tn), lambda i, j: (0, j))
        o_spec = pl.BlockSpec((tm, tn), lambda i, j: (i, j))
    return pl.pallas_call(
        _mm_kernel,
        grid=grid,
        in_specs=[a_spec, b_spec],
        out_specs=o_spec,
        out_shape=jax.ShapeDtypeStruct((m, n), out_dtype),
        compiler_params=_params("parallel", "arbitrary"),
        name=name,
    )(a, b)


def _mm2_kernel(a1_ref, a2_ref, b_ref, o_ref):
    k1 = a1_ref.shape[1]
    o_ref[...] = (_dot(a1_ref[...], b_ref[:k1, :]) + _dot(a2_ref[...], b_ref[k1:, :])).astype(o_ref.dtype)


def matmul_concat(a1, a2, b, out_dtype, tm_pref, tn_pref, name):
    m, k1 = a1.shape
    k2 = a2.shape[1]
    n = b.shape[1]
    tm = _tile(m, tm_pref, 8)
    tn = _tile(n, tn_pref, 128)
    return pl.pallas_call(
        _mm2_kernel,
        grid=(m // tm, n // tn),
        in_specs=[pl.BlockSpec((tm, k1), lambda i, j: (i, 0)),
                  pl.BlockSpec((tm, k2), lambda i, j: (i, 0)),
                  pl.BlockSpec((k1 + k2, tn), lambda i, j: (0, j))],
        out_specs=pl.BlockSpec((tm, tn), lambda i, j: (i, j)),
        out_shape=jax.ShapeDtypeStruct((m, n), out_dtype),
        compiler_params=_params("parallel", "arbitrary"),
        name=name,
    )(a1, a2, b)


def _hgrn_consts(chunk, reverse):
    nl = int(math.log2(chunk))
    assert 1 << nl == chunk
    t = np.arange(chunk)
    u = t[None, :]
    level_sums = np.zeros((nl, chunk, chunk), np.float32)
    qside = np.zeros((nl, chunk), np.float32)
    amask = np.zeros((nl + 1, chunk, chunk), np.float32)
    for l in range(nl):
        half = 1 << l
        blk = t // (2 * half)
        mid = blk * 2 * half + half
        upper = t >= mid
        level_sums[l] = np.where(upper[:, None],
                                 (u >= mid[:, None]) & (u <= t[:, None]),
                                 (u > t[:, None]) & (u <= mid[:, None] - 1))
        qside[l] = upper
        amask[l] = (blk[:, None] == blk[None, :]) & upper[:, None] & ~upper[None, :]
    amask[nl] = np.eye(chunk)
    sums = np.stack([level_sums[l] for l in HG_TABLE_LEVELS] + [(u <= t[:, None]).astype(np.float32)])
    if reverse:
        sums = sums[:, ::-1, ::-1]
        qside = qside[:, ::-1]
        amask = amask[:, ::-1, ::-1]
    sums = np.ascontiguousarray(sums).reshape(len(sums) * chunk, chunk)
    sums = np.concatenate([sums, sums], axis=1)
    qside = 2.0 * qside - 1.0
    qside = np.ascontiguousarray(np.broadcast_to(qside[:, :, None], (nl, chunk, HEAD_DIM)))
    return (jnp.asarray(sums, BF16), jnp.asarray(qside, F32),
            jnp.asarray(np.ascontiguousarray(amask), F32))


def _block_reference(cum, level, reverse):
    chunk = cum.shape[0]
    half = 1 << level
    pieces = []
    for start in range(0, chunk, 2 * half):
        row = start + half if reverse else start + half - 1
        pieces.append(jnp.broadcast_to(cum[row:row + 1, :], (2 * half, cum.shape[1])))
    return pieces[0] if len(pieces) == 1 else jnp.concatenate(pieces, axis=0)


def _run_interleaved(stage_generators):
    live = list(stage_generators)
    while live:
        for gen in list(live):
            try:
                next(gen)
            except StopIteration:
                live.remove(gen)


def _hgrn_kernel(hq_ref, ff_ref, fb_ref, hi_ref, hg_ref, lb_ref, hn_ref, *rest, chunk, nl):
    consts_f, consts_b = rest[:3], rest[3:6]
    out_ref, of_ref, ob_ref, stf_ref, stb_ref = rest[6:]
    seq = hq_ref.shape[0]
    nchunks = seq // chunk
    stf_ref[...] = jnp.zeros_like(stf_ref)
    stb_ref[...] = jnp.zeros_like(stb_ref)

    def chunk_step(base, direction, f_ref, consts, st_ref, o_ref):
        sums_ref, side_ref, am_ref = consts
        rows = pl.ds(pl.multiple_of(base, chunk), chunk)
        q = _silu(hq_ref[rows, :])
        lb = lb_ref[direction:direction + 1, :]
        f = lb + (1.0 - lb) * _sigmoid(f_ref[rows, :])
        k = 1.0 - f
        g = jnp.log(f)
        v = hi_ref[rows, :].astype(BF16)
        g1 = g.astype(BF16)
        g2 = (g - g1.astype(F32)).astype(BF16)
        z = _dot(sums_ref[...], jnp.concatenate([g1, g2], axis=0))
        yield
        cum = z[len(HG_TABLE_LEVELS) * chunk:]
        a = am_ref[nl] * _dot_nt(q.astype(BF16), k.astype(BF16))
        for l in range(nl):
            side = side_ref[l]
            qside = side > 0.0
            if l == 0:
                x = jnp.where(qside, q * f, k)
            else:
                if l in HG_TABLE_LEVELS:
                    i = HG_TABLE_LEVELS.index(l)
                    zl = z[i * chunk:(i + 1) * chunk]
                else:
                    zl = (cum - _block_reference(cum, l, bool(direction))) * side
                x = jnp.where(qside, q, k) * jnp.exp(zl)
            a = a + am_ref[l] * _dot_nt(x.astype(BF16), x.astype(BF16))
            yield
        last = 0 if direction else chunk - 1
        cum_last = cum[last:last + 1, :]
        qd = (q * jnp.exp(cum)).astype(BF16)
        kd = (k * jnp.exp(cum_last - cum)).astype(BF16)
        yield
        st = st_ref[...]
        o_ref[rows, :] = _dot(a.astype(BF16), v) + _dot_nt(qd, st.astype(BF16))
        st_ref[...] = st * jnp.exp(cum_last) + _dot_tn(v, kd)

    def finish(base):
        rows = pl.ds(pl.multiple_of(base, chunk), chunk)
        o = of_ref[rows, :] + ob_ref[rows, :]
        inv = lax.rsqrt(jnp.mean(o * o, axis=-1, keepdims=True) + EPS)
        out_ref[rows, :] = (o * inv * hn_ref[...] * _silu(hg_ref[rows, :])).astype(out_ref.dtype)

    def body(n, carry, both_done):
        fwd_base, bwd_base = n * chunk, (nchunks - 1 - n) * chunk
        _run_interleaved([chunk_step(fwd_base, 0, ff_ref, consts_f, stf_ref, of_ref),
                          chunk_step(bwd_base, 1, fb_ref, consts_b, stb_ref, ob_ref)])
        if both_done:
            finish(fwd_base)
            finish(bwd_base)
        return carry

    crossed = nchunks // 2
    lax.fori_loop(0, crossed, functools.partial(body, both_done=False), 0,
                  unroll=max(1, min(HG_CHUNK_UNROLL, crossed)))
    lax.fori_loop(crossed, nchunks, functools.partial(body, both_done=True), 0,
                  unroll=min(HG_CHUNK_UNROLL, nchunks - crossed))


def hgrn2(proj, lb, head_norm, n_heads):
    b, t, _ = proj.shape
    chunk = min(HG_CHUNK, t)
    nl = int(math.log2(chunk))
    consts_f = _hgrn_consts(chunk, False)
    consts_b = _hgrn_consts(chunk, True)

    def col(group):
        return pl.BlockSpec((None, t, HEAD_DIM), lambda bi, hi, g=group: (bi, 0, g * n_heads + hi))

    def const(arr):
        return pl.BlockSpec(arr.shape, lambda bi, hi, nd=arr.ndim: (0,) * nd)

    return pl.pallas_call(
        functools.partial(_hgrn_kernel, chunk=chunk, nl=nl),
        grid=(b, n_heads),
        in_specs=[col(0), col(1), col(2), col(3), col(4),
                  pl.BlockSpec((2, HEAD_DIM), lambda bi, hi: (0, hi)),
                  pl.BlockSpec((1, HEAD_DIM), lambda bi, hi: (0, 0)),
                  *[const(c) for c in consts_f], *[const(c) for c in consts_b]],
        out_specs=pl.BlockSpec((None, t, HEAD_DIM), lambda bi, hi: (bi, 0, hi)),
        out_shape=jax.ShapeDtypeStruct((b, t, n_heads * HEAD_DIM), BF16),
        scratch_shapes=[pltpu.VMEM((t, HEAD_DIM), F32), pltpu.VMEM((t, HEAD_DIM), F32),
                        pltpu.VMEM((HEAD_DIM, HEAD_DIM), F32), pltpu.VMEM((HEAD_DIM, HEAD_DIM), F32)],
        compiler_params=_params("parallel", "parallel"),
        name="hgrn2",
    )(proj, proj, proj, proj, proj, lb, head_norm.reshape(1, HEAD_DIM), *consts_f, *consts_b)


def _na_col_mask():
    col = np.arange(GRID_W)
    col_start = np.clip(col - NA_WIN_COLS // 2, 0, GRID_W - NA_WIN_COLS)
    inside = (col[None, :] >= col_start[:, None]) & (col[None, :] < col_start[:, None] + NA_WIN_COLS)
    return jnp.asarray(np.tile(np.where(inside, 0.0, NEG_BIG), (1, 128 // GRID_W)), F32)


def _na_build_bias(rpb_ref, cmask_ref, tab_ref, bias_ref):
    lane = lax.broadcasted_iota(jnp.int32, (GRID_W, 128), 1)
    first_half = lane < GRID_W
    back = NA_WIN_COLS - 1
    n_dr = 2 * NA_WIN_ROWS - 1
    for dr in range(n_dr):
        row = jnp.broadcast_to(rpb_ref[dr:dr + 1, :], (GRID_W, 128))
        lo = pltpu.roll(row, 128 - back, 1, stride=1, stride_axis=0)
        hi = pltpu.roll(row, GRID_W - back, 1, stride=1, stride_axis=0)
        tab_ref[dr] = jnp.where(first_half, lo, hi) + cmask_ref[...]
    tab_ref[n_dr] = jnp.full((GRID_W, 128), NEG_BIG, F32)
    for variant, delta in enumerate((0, -NA_GROUP, -2 * NA_GROUP)):
        for i in range(NA_GROUP):
            win_lo = (0, i, NA_GROUP)[variant]

            def tile(j):
                inside = win_lo <= j < win_lo + NA_WIN_ROWS
                return tab_ref[delta + j - i + NA_WIN_ROWS - 1 if inside else n_dr]

            for pair in range(NA_SLAB_ROWS // 2):
                bias_ref[variant, i * GRID_W:(i + 1) * GRID_W, pair * 128:(pair + 1) * 128] = jnp.where(
                    first_half, tile(2 * pair), tile(2 * pair + 1))


def _na_kernel(q_ref, k_ref, v_ref, rpb_ref, cmask_ref, out_ref, kb_ref, vb_ref, tab_ref, bias_ref, *, scale):
    seq = q_ref.shape[0]
    n_groups = seq // GRID_W // NA_GROUP
    group = NA_GROUP * GRID_W
    slab = NA_SLAB_ROWS * GRID_W
    kb_ref[...] = k_ref[...].astype(BF16)
    vb_ref[...] = v_ref[...].astype(BF16)
    _na_build_bias(rpb_ref, cmask_ref, tab_ref, bias_ref)

    def group_step(gi):
        slab_start = jnp.clip(gi - 1, 0, n_groups - NA_SLAB_ROWS // NA_GROUP) * group
        variant = jnp.where(gi == 0, 0, jnp.where(gi == n_groups - 1, 2, 1))
        qrows = pl.ds(pl.multiple_of(gi * group, group), group)
        krows = pl.ds(pl.multiple_of(slab_start, group), slab)
        q = (q_ref[qrows, :] * scale).astype(BF16)
        s = _dot_nt(q, kb_ref[krows, :]) + bias_ref[variant]
        yield
        m = jnp.max(s, axis=-1, keepdims=True)
        p = jnp.exp(s - m)
        yield
        denom = jnp.sum(p, axis=-1, keepdims=True)
        o = _dot(p.astype(BF16), vb_ref[krows, :]) / denom
        yield
        out_ref[qrows, :] = o.astype(out_ref.dtype)

    def body(pi, carry):
        _run_interleaved([group_step(pi * NA_INTERLEAVE + c) for c in range(NA_INTERLEAVE)])
        return carry

    assert n_groups % NA_INTERLEAVE == 0
    lax.fori_loop(0, n_groups // NA_INTERLEAVE, body, 0, unroll=max(1, NA_GROUP_UNROLL // NA_INTERLEAVE))


def neighbourhood_attention(proj, rpb, n_heads, col0):
    b, t, _ = proj.shape
    n_rows = t // GRID_W
    assert NA_GROUP == NA_WIN_ROWS // 2 and NA_SLAB_ROWS % NA_GROUP == 0 and (NA_SLAB_ROWS * GRID_W) % 128 == 0
    assert t % GRID_W == 0 and n_rows % NA_GROUP == 0 and n_rows >= NA_SLAB_ROWS + NA_GROUP
    n_dr, n_dc = rpb.shape[1:]
    assert (n_dr, n_dc) == (2 * NA_WIN_ROWS - 1, 2 * NA_WIN_COLS - 1)
    rpb_pad = jnp.pad(rpb.astype(F32), ((0, 0), (0, 1), (0, 128 - n_dc)))
    cmask = _na_col_mask()

    def col(group):
        return pl.BlockSpec((None, t, HEAD_DIM), lambda bi, hi, g=group: (bi, 0, col0 + g * n_heads + hi))

    return pl.pallas_call(
        functools.partial(_na_kernel, scale=HEAD_DIM ** -0.5),
        grid=(b, n_heads),
        in_specs=[col(0), col(1), col(2),
                  pl.BlockSpec((None, n_dr + 1, 128), lambda bi, hi: (hi, 0, 0)),
                  pl.BlockSpec(cmask.shape, lambda bi, hi: (0, 0))],
        out_specs=pl.BlockSpec((None, t, HEAD_DIM), lambda bi, hi: (bi, 0, hi)),
        out_shape=jax.ShapeDtypeStruct((b, t, n_heads * HEAD_DIM), BF16),
        scratch_shapes=[pltpu.VMEM((t, HEAD_DIM), BF16), pltpu.VMEM((t, HEAD_DIM), BF16),
                        pltpu.VMEM((n_dr + 1, GRID_W, 128), F32),
                        pltpu.VMEM((3, NA_GROUP * GRID_W, NA_SLAB_ROWS * GRID_W), F32)],
        compiler_params=_params("parallel", "parallel"),
        name="neighbourhood_attention",
    )(proj, proj, proj, rpb_pad, cmask)


def _swiglu_ffn(u, w_gate, w_up, w_down):
    act = gateup(u, w_gate, w_up)
    return matmul(act, w_down.astype(BF16), BF16, 512, 1024, "ffn_down", rhs_resident=True)


def kernel(x, ffn1_norm_pre, ffn1_w_gate, ffn1_w_up, ffn1_w_down, ffn1_norm_post, mix_norm_pre, w_in, hgrn_lb_logits, hgrn_head_norm, na_rpb, w_out, mix_norm_post, ffn2_norm_pre, ffn2_w_gate, ffn2_w_up, ffn2_w_down, ffn2_norm_post):
    b, t, d = x.shape
    depth = w_in.shape[0]
    hg_width = hgrn_lb_logits.shape[-1]
    hg_heads = hg_width // HEAD_DIM
    na_heads = na_rpb.shape[1]
    h = x.reshape(b * t, d)
    u = rmsnorm_cast(h, ffn1_norm_pre[0])
    for layer in range(depth):
        ff = _swiglu_ffn(u, ffn1_w_gate[layer], ffn1_w_up[layer], ffn1_w_down[layer])
        h, u = resnorm(h, ff, ffn1_norm_post[layer], FFN_RES_WEIGHT, mix_norm_pre[layer])

        proj = matmul(u, w_in[layer], F32, 2048, 512, "in_proj").reshape(b, t, -1)
        lb = jnp.cumsum(jax.nn.softmax(hgrn_lb_logits.astype(F32), axis=0), axis=0)[layer]
        o_hg = hgrn2(proj, lb, hgrn_head_norm[layer], hg_heads)
        o_na = neighbourhood_attention(proj, na_rpb[layer], na_heads, 5 * hg_heads)
        mixed = matmul_concat(o_hg.reshape(b * t, -1), o_na.reshape(b * t, -1), w_out[layer].astype(BF16),
                              BF16, 1024, 512, "out_proj")
        h, u = resnorm(h, mixed, mix_norm_post[layer], 1.0, ffn2_norm_pre[layer])

        ff = _swiglu_ffn(u, ffn2_w_gate[layer], ffn2_w_up[layer], ffn2_w_down[layer])
        if layer + 1 < depth:
            h, u = resnorm(h, ff, ffn2_norm_post[layer], FFN_RES_WEIGHT, ffn1_norm_pre[layer + 1])
        else:
            h = resnorm(h, ff, ffn2_norm_post[layer], FFN_RES_WEIGHT)
    return h.reshape(b, t, d)
```

```python
import functools
import math

import numpy as np
import jax
import jax.numpy as jnp
from jax import lax
from jax.experimental import pallas as pl
from jax.experimental.pallas import tpu as pltpu

EPS = 1e-6
HEAD_DIM = 128
GRID_W = 64
NA_WIN_ROWS = 8
NA_WIN_COLS = 16
HG_CHUNK = 128
HG_TILE_LEVEL = 3
HG_CHUNK_UNROLL = 8
NA_GROUP = 4
NA_SLAB_ROWS = NA_WIN_ROWS + NA_GROUP
NA_GROUP_UNROLL = 4
NA_INTERLEAVE = 4
FFN_RES_WEIGHT = 0.5
LOG2E = 1.0 / math.log(2.0)
NEG_BIG = -1e30
V7X_VMEM_LIMIT_BYTES = 56 * 1024 * 1024

BF16 = jnp.bfloat16
F32 = jnp.float32


def _params(*sem):
    return pltpu.CompilerParams(dimension_semantics=sem, vmem_limit_bytes=V7X_VMEM_LIMIT_BYTES)


def _dot(a, b):
    return jnp.dot(a, b, preferred_element_type=F32)


def _dot_nt(a, b):
    return lax.dot_general(a, b, (((1,), (1,)), ((), ())), preferred_element_type=F32)


def _dot_tn(a, b):
    return lax.dot_general(a, b, (((0,), (0,)), ((), ())), preferred_element_type=F32)


def _sigmoid(x):
    return 1.0 / (1.0 + jnp.exp2(x * -LOG2E))


def _silu(x):
    return x * _sigmoid(x)


def _tile(n, pref, mult):
    t = (min(pref, n) // mult) * mult
    while t >= mult:
        if n % t == 0:
            return t
        t -= mult
    return n


def _rmsnorm_cast_kernel(x_ref, g_ref, o_ref):
    x = x_ref[...]
    inv = lax.rsqrt(jnp.mean(x * x, axis=-1, keepdims=True) + EPS)
    o_ref[...] = (x * inv * g_ref[...]).astype(o_ref.dtype)


def rmsnorm_cast(x, gain):
    m, d = x.shape
    tm = _tile(m, 256, 8)
    return pl.pallas_call(
        _rmsnorm_cast_kernel,
        grid=(m // tm,),
        in_specs=[pl.BlockSpec((tm, d), lambda i: (i, 0)), pl.BlockSpec((1, d), lambda i: (0, 0))],
        out_specs=pl.BlockSpec((tm, d), lambda i: (i, 0)),
        out_shape=jax.ShapeDtypeStruct((m, d), BF16),
        compiler_params=_params("parallel"),
        name="rmsnorm_cast",
    )(x, gain.reshape(1, d))


def _resnorm_kernel(h_ref, y_ref, gp_ref, gn_ref, hn_ref, u_ref, *, weight):
    y = y_ref[...].astype(F32)
    inv = lax.rsqrt(jnp.mean(y * y, axis=-1, keepdims=True) + EPS)
    h = h_ref[...] + weight * (y * inv * gp_ref[...])
    hn_ref[...] = h
    inv2 = lax.rsqrt(jnp.mean(h * h, axis=-1, keepdims=True) + EPS)
    u_ref[...] = (h * inv2 * gn_ref[...]).astype(u_ref.dtype)


def _resnorm_last_kernel(h_ref, y_ref, gp_ref, hn_ref, *, weight):
    y = y_ref[...].astype(F32)
    inv = lax.rsqrt(jnp.mean(y * y, axis=-1, keepdims=True) + EPS)
    hn_ref[...] = h_ref[...] + weight * (y * inv * gp_ref[...])


def resnorm(h, y, gain_post, weight, gain_next=None):
    m, d = h.shape
    tm = _tile(m, 256, 8)
    row = pl.BlockSpec((tm, d), lambda i: (i, 0))
    vec = pl.BlockSpec((1, d), lambda i: (0, 0))
    if gain_next is None:
        return pl.pallas_call(
            functools.partial(_resnorm_last_kernel, weight=weight),
            grid=(m // tm,),
            in_specs=[row, row, vec],
            out_specs=row,
            out_shape=jax.ShapeDtypeStruct((m, d), F32),
            compiler_params=_params("parallel"),
            name="resnorm_last",
        )(h, y, gain_post.reshape(1, d))
    return pl.pallas_call(
        functools.partial(_resnorm_kernel, weight=weight),
        grid=(m // tm,),
        in_specs=[row, row, vec, vec],
        out_specs=[row, row],
        out_shape=[jax.ShapeDtypeStruct((m, d), F32), jax.ShapeDtypeStruct((m, d), BF16)],
        compiler_params=_params("parallel"),
        name="resnorm",
    )(h, y, gain_post.reshape(1, d), gain_next.reshape(1, d))


def _gateup_kernel(x_ref, wg_ref, wu_ref, o_ref):
    x = x_ref[...]
    g = _dot(x, wg_ref[...].astype(BF16))
    u = _dot(x, wu_ref[...].astype(BF16))
    o_ref[...] = (_silu(g) * u).astype(o_ref.dtype)


def gateup(x, wg, wu):
    m, k = x.shape
    n = wg.shape[1]
    tm = _tile(m, 1024, 8)
    tn = _tile(n, 256, 128)
    return pl.pallas_call(
        _gateup_kernel,
        grid=(m // tm, n // tn),
        in_specs=[pl.BlockSpec((tm, k), lambda i, j: (i, 0)),
                  pl.BlockSpec((k, tn), lambda i, j: (0, j)),
                  pl.BlockSpec((k, tn), lambda i, j: (0, j))],
        out_specs=pl.BlockSpec((tm, tn), lambda i, j: (i, j)),
        out_shape=jax.ShapeDtypeStruct((m, n), BF16),
        compiler_params=_params("parallel", "arbitrary"),
        name="ffn_gateup",
    )(x, wg, wu)


def _mm_kernel(a_ref, b_ref, o_ref):
    o_ref[...] = _dot(a_ref[...], b_ref[...].astype(BF16)).astype(o_ref.dtype)


def matmul(a, b, out_dtype, tm_pref, tn_pref, name, rhs_resident=False):
    m, k = a.shape
    n = b.shape[1]
    tm = _tile(m, tm_pref, 8)
    tn = _tile(n, tn_pref, 128)
    once = pl.Buffered(1)
    if rhs_resident:
        grid = (n // tn, m // tm)
        a_spec = pl.BlockSpec((tm, k), lambda j, i: (i, 0))
        b_spec = pl.BlockSpec((k, tn), lambda j, i: (0, j), pipeline_mode=once)
        o_spec = pl.BlockSpec((tm, tn), lambda j, i: (i, j))
    else:
        grid = (m // tm, n // tn)
        a_spec = pl.BlockSpec((tm, k), lambda i, j: (i, 0), pipeline_mode=once)
        b_spec = pl.BlockSpec((k, tn), lambda i, j: (0, j))
        o_spec = pl.BlockSpec((tm, tn), lambda i, j: (i, j))
    return pl.pallas_call(
        _mm_kernel,
        grid=grid,
        in_specs=[a_spec, b_spec],
        out_specs=o_spec,
        out_shape=jax.ShapeDtypeStruct((m, n), out_dtype),
        compiler_params=_params("parallel", "arbitrary"),
        name=name,
    )(a, b)


def _mm2_kernel(a1_ref, a2_ref, b_ref, o_ref):
    k1 = a1_ref.shape[1]
    o_ref[...] = (_dot(a1_ref[...], b_ref[:k1, :]) + _dot(a2_ref[...], b_ref[k1:, :])).astype(o_ref.dtype)


def matmul_concat(a1, a2, b, out_dtype, tm_pref, tn_pref, name):
    m, k1 = a1.shape
    k2 = a2.shape[1]
    n = b.shape[1]
    tm = _tile(m, tm_pref, 8)
    tn = _tile(n, tn_pref, 128)
    return pl.pallas_call(
        _mm2_kernel,
        grid=(m // tm, n // tn),
        in_specs=[pl.BlockSpec((tm, k1), lambda i, j: (i, 0)),
                  pl.BlockSpec((tm, k2), lambda i, j: (i, 0)),
                  pl.BlockSpec((k1 + k2, tn), lambda i, j: (0, j))],
        out_specs=pl.BlockSpec((tm, tn), lambda i, j: (i, j)),
        out_shape=jax.ShapeDtypeStruct((m, n), out_dtype),
        compiler_params=_params("parallel", "arbitrary"),
        name=name,
    )(a1, a2, b)


def _hgrn_consts(chunk, reverse):
    nl = int(math.log2(chunk))
    assert 1 << nl == chunk and nl > HG_TILE_LEVEL
    t = np.arange(chunk)
    u = t[None, :]
    level_sums = np.zeros((nl, chunk, chunk), np.float32)
    qside = np.zeros((nl, chunk), bool)
    amask = np.zeros((nl + 1, chunk, chunk), np.float32)
    for l in range(nl):
        half = 1 << l
        blk = t // (2 * half)
        mid = blk * 2 * half + half
        upper = t >= mid
        level_sums[l] = np.where(upper[:, None],
                                 (u >= mid[:, None]) & (u <= t[:, None]),
                                 (u > t[:, None]) & (u <= mid[:, None] - 1))
        qside[l] = upper
        amask[l] = (blk[:, None] == blk[None, :]) & upper[:, None] & ~upper[None, :]
    amask[nl] = np.eye(chunk)
    sums = np.stack([level_sums[l] for l in range(1, HG_TILE_LEVEL)] + [(u <= t[:, None]).astype(np.float32)])
    if reverse:
        sums = sums[:, ::-1, ::-1]
        qside = qside[:, ::-1]
        amask = amask[:, ::-1, ::-1]
    sums = np.ascontiguousarray(sums).reshape(len(sums) * chunk, chunk)
    sums = np.concatenate([sums, sums], axis=1)
    side = np.where(qside[:HG_TILE_LEVEL], 1.0, -1.0)
    side = np.ascontiguousarray(np.broadcast_to(side[:, :, None], (HG_TILE_LEVEL, chunk, HEAD_DIM)))
    am_low = np.concatenate([amask[:HG_TILE_LEVEL], amask[nl:]])
    am_high = np.stack([amask[l][qside[l]] for l in range(HG_TILE_LEVEL, nl)])
    return (jnp.asarray(sums, BF16), jnp.asarray(side, F32),
            jnp.asarray(np.ascontiguousarray(am_low), F32), jnp.asarray(np.ascontiguousarray(am_high), F32))


def _run_interleaved(stage_generators):
    live = list(stage_generators)
    while live:
        for gen in list(live):
            try:
                next(gen)
            except StopIteration:
                live.remove(gen)


def _hgrn_kernel(hq_ref, ff_ref, fb_ref, hi_ref, hg_ref, lb_ref, hn_ref, *rest, chunk, nl):
    consts_f, consts_b = rest[:4], rest[4:8]
    out_ref, of_ref, ob_ref, stf_ref, stb_ref = rest[8:]
    seq = hq_ref.shape[0]
    nchunks = seq // chunk
    stf_ref[...] = jnp.zeros_like(stf_ref)
    stb_ref[...] = jnp.zeros_like(stb_ref)

    def chunk_step(base, direction, f_ref, consts, st_ref, o_ref):
        sums_ref, side_ref, am_low_ref, am_high_ref = consts
        reverse = bool(direction)
        rows = pl.ds(pl.multiple_of(base, chunk), chunk)
        q = _silu(hq_ref[rows, :])
        lb = lb_ref[direction:direction + 1, :]
        f = lb + (1.0 - lb) * _sigmoid(f_ref[rows, :])
        k = 1.0 - f
        g = jnp.log(f) * LOG2E
        v = hi_ref[rows, :].astype(BF16)
        g1 = g.astype(BF16)
        g2 = (g - g1.astype(F32)).astype(BF16)
        z = _dot(sums_ref[...], jnp.concatenate([g1, g2], axis=0))
        yield
        cum = z[(HG_TILE_LEVEL - 1) * chunk:]

        a = am_low_ref[HG_TILE_LEVEL] * _dot_nt(q.astype(BF16), k.astype(BF16))
        for l in range(HG_TILE_LEVEL):
            qside = side_ref[l] > 0.0
            if l == 0:
                x = jnp.where(qside, q * f, k)
            else:
                x = jnp.where(qside, q, k) * jnp.exp2(z[(l - 1) * chunk:l * chunk])
            x = x.astype(BF16)
            a = a + am_low_ref[l] * _dot_nt(x, x)
            yield

        a_tiles = [a[8 * p:8 * p + 8] for p in range(chunk // 8)]
        for l in range(HG_TILE_LEVEL, nl):
            half = 1 << l
            x_parts, q_parts, q_tiles = [], [], []
            for start in range(0, chunk, 2 * half):
                mid = start + half
                ref = cum[mid:mid + 1] if reverse else cum[mid - 1:mid]
                for lo, is_q in ((start, reverse), (mid, not reverse)):
                    if is_q:
                        part = q[lo:lo + half] * jnp.exp2(cum[lo:lo + half] - ref)
                        q_parts.append(part)
                        q_tiles.extend(range(lo // 8, (lo + half) // 8))
                    else:
                        part = k[lo:lo + half] * jnp.exp2(ref - cum[lo:lo + half])
                    x_parts.append(part)
            x = jnp.concatenate(x_parts, axis=0).astype(BF16)
            xq = (q_parts[0] if len(q_parts) == 1 else jnp.concatenate(q_parts, axis=0)).astype(BF16)
            scores = am_high_ref[l - HG_TILE_LEVEL] * _dot_nt(xq, x)
            for i, p in enumerate(q_tiles):
                a_tiles[p] = a_tiles[p] + scores[8 * i:8 * i + 8]
            yield
        a = jnp.concatenate(a_tiles, axis=0)

        last = 0 if reverse else chunk - 1
        cum_last = cum[last:last + 1, :]
        qd = (q * jnp.exp2(cum)).astype(BF16)
        kd = (k * jnp.exp2(cum_last - cum)).astype(BF16)
        yield
        st = st_ref[...]
        o_ref[rows, :] = _dot(a.astype(BF16), v) + _dot_nt(qd, st.astype(BF16))
        st_ref[...] = st * jnp.exp2(cum_last) + _dot_tn(v, kd)

    def finish(base):
        rows = pl.ds(pl.multiple_of(base, chunk), chunk)
        o = of_ref[rows, :] + ob_ref[rows, :]
        inv = lax.rsqrt(jnp.mean(o * o, axis=-1, keepdims=True) + EPS)
        out_ref[rows, :] = (o * inv * hn_ref[...] * _silu(hg_ref[rows, :])).astype(out_ref.dtype)

    def body(n, carry, both_done):
        fwd_base, bwd_base = n * chunk, (nchunks - 1 - n) * chunk
        _run_interleaved([chunk_step(fwd_base, 0, ff_ref, consts_f, stf_ref, of_ref),
                          chunk_step(bwd_base, 1, fb_ref, consts_b, stb_ref, ob_ref)])
        if both_done:
            finish(fwd_base)
            finish(bwd_base)
        return carry

    crossed = nchunks // 2
    lax.fori_loop(0, crossed, functools.partial(body, both_done=False), 0,
                  unroll=max(1, min(HG_CHUNK_UNROLL, crossed)))
    lax.fori_loop(crossed, nchunks, functools.partial(body, both_done=True), 0,
                  unroll=min(HG_CHUNK_UNROLL, nchunks - crossed))


def hgrn2(proj, lb, head_norm, n_heads):
    b, t, _ = proj.shape
    chunk = min(HG_CHUNK, t)
    nl = int(math.log2(chunk))
    consts_f = _hgrn_consts(chunk, False)
    consts_b = _hgrn_consts(chunk, True)

    def col(group):
        return pl.BlockSpec((None, t, HEAD_DIM), lambda bi, hi, g=group: (bi, 0, g * n_heads + hi))

    def const(arr):
        return pl.BlockSpec(arr.shape, lambda bi, hi, nd=arr.ndim: (0,) * nd)

    return pl.pallas_call(
        functools.partial(_hgrn_kernel, chunk=chunk, nl=nl),
        grid=(b, n_heads),
        in_specs=[col(0), col(1), col(2), col(3), col(4),
                  pl.BlockSpec((2, HEAD_DIM), lambda bi, hi: (0, hi)),
                  pl.BlockSpec((1, HEAD_DIM), lambda bi, hi: (0, 0)),
                  *[const(c) for c in consts_f], *[const(c) for c in consts_b]],
        out_specs=pl.BlockSpec((None, t, HEAD_DIM), lambda bi, hi: (bi, 0, hi)),
        out_shape=jax.ShapeDtypeStruct((b, t, n_heads * HEAD_DIM), BF16),
        scratch_shapes=[pltpu.VMEM((t, HEAD_DIM), F32), pltpu.VMEM((t, HEAD_DIM), F32),
                        pltpu.VMEM((HEAD_DIM, HEAD_DIM), F32), pltpu.VMEM((HEAD_DIM, HEAD_DIM), F32)],
        compiler_params=_params("parallel", "parallel"),
        name="hgrn2",
    )(proj, proj, proj, proj, proj, lb, head_norm.reshape(1, HEAD_DIM), *consts_f, *consts_b)


def _na_col_mask():
    col = np.arange(GRID_W)
    col_start = np.clip(col - NA_WIN_COLS // 2, 0, GRID_W - NA_WIN_COLS)
    inside = (col[None, :] >= col_start[:, None]) & (col[None, :] < col_start[:, None] + NA_WIN_COLS)
    return jnp.asarray(np.tile(np.where(inside, 0.0, NEG_BIG), (1, 128 // GRID_W)), F32)


def _na_build_bias(rpb_ref, cmask_ref, tab_ref, bias_ref):
    lane = lax.broadcasted_iota(jnp.int32, (GRID_W, 128), 1)
    first_half = lane < GRID_W
    back = NA_WIN_COLS - 1
    n_dr = 2 * NA_WIN_ROWS - 1
    for dr in range(n_dr):
        row = jnp.broadcast_to(rpb_ref[dr:dr + 1, :] * LOG2E, (GRID_W, 128))
        lo = pltpu.roll(row, 128 - back, 1, stride=1, stride_axis=0)
        hi = pltpu.roll(row, GRID_W - back, 1, stride=1, stride_axis=0)
        tab_ref[dr] = jnp.where(first_half, lo, hi) + cmask_ref[...]
    tab_ref[n_dr] = jnp.full((GRID_W, 128), NEG_BIG, F32)
    for variant, delta in enumerate((0, -NA_GROUP, -2 * NA_GROUP)):
        for i in range(NA_GROUP):
            win_lo = (0, i, NA_GROUP)[variant]

            def tile(j):
                inside = win_lo <= j < win_lo + NA_WIN_ROWS
                return tab_ref[delta + j - i + NA_WIN_ROWS - 1 if inside else n_dr]

            for pair in range(NA_SLAB_ROWS // 2):
                bias_ref[variant, i * GRID_W:(i + 1) * GRID_W, pair * 128:(pair + 1) * 128] = jnp.where(
                    first_half, tile(2 * pair), tile(2 * pair + 1))


def _na_kernel(q_ref, k_ref, v_ref, rpb_ref, cmask_ref, out_ref, kb_ref, vb_ref, tab_ref, bias_ref, *, scale):
    seq = q_ref.shape[0]
    n_groups = seq // GRID_W // NA_GROUP
    group = NA_GROUP * GRID_W
    slab = NA_SLAB_ROWS * GRID_W
    kb_ref[...] = k_ref[...].astype(BF16)
    vb_ref[...] = v_ref[...].astype(BF16)
    _na_build_bias(rpb_ref, cmask_ref, tab_ref, bias_ref)

    def group_step(gi):
        slab_start = jnp.clip(gi - 1, 0, n_groups - NA_SLAB_ROWS // NA_GROUP) * group
        variant = jnp.where(gi == 0, 0, jnp.where(gi == n_groups - 1, 2, 1))
        qrows = pl.ds(pl.multiple_of(gi * group, group), group)
        krows = pl.ds(pl.multiple_of(slab_start, group), slab)
        q = (q_ref[qrows, :] * scale).astype(BF16)
        s = _dot_nt(q, kb_ref[krows, :]) + bias_ref[variant]
        yield
        m = jnp.max(s, axis=-1, keepdims=True)
        p = jnp.exp2(s - m)
        yield
        denom = jnp.sum(p, axis=-1, keepdims=True)
        o = _dot(p.astype(BF16), vb_ref[krows, :]) / denom
        yield
        out_ref[qrows, :] = o.astype(out_ref.dtype)

    def body(pi, carry):
        _run_interleaved([group_step(pi * NA_INTERLEAVE + c) for c in range(NA_INTERLEAVE)])
        return carry

    assert n_groups % NA_INTERLEAVE == 0
    lax.fori_loop(0, n_groups // NA_INTERLEAVE, body, 0, unroll=max(1, NA_GROUP_UNROLL // NA_INTERLEAVE))


def neighbourhood_attention(proj, rpb, n_heads, col0):
    b, t, _ = proj.shape
    n_rows = t // GRID_W
    assert NA_GROUP == NA_WIN_ROWS // 2 and NA_SLAB_ROWS % NA_GROUP == 0 and (NA_SLAB_ROWS * GRID_W) % 128 == 0
    assert t % GRID_W == 0 and n_rows % NA_GROUP == 0 and n_rows >= NA_SLAB_ROWS + NA_GROUP
    n_dr, n_dc = rpb.shape[1:]
    assert (n_dr, n_dc) == (2 * NA_WIN_ROWS - 1, 2 * NA_WIN_COLS - 1)
    rpb_pad = jnp.pad(rpb.astype(F32), ((0, 0), (0, 1), (0, 128 - n_dc)))
    cmask = _na_col_mask()

    def col(group):
        return pl.BlockSpec((None, t, HEAD_DIM), lambda bi, hi, g=group: (bi, 0, col0 + g * n_heads + hi))

    return pl.pallas_call(
        functools.partial(_na_kernel, scale=HEAD_DIM ** -0.5 * LOG2E),
        grid=(b, n_heads),
        in_specs=[col(0), col(1), col(2),
                  pl.BlockSpec((None, n_dr + 1, 128), lambda bi, hi: (hi, 0, 0)),
                  pl.BlockSpec(cmask.shape, lambda bi, hi: (0, 0))],
        out_specs=pl.BlockSpec((None, t, HEAD_DIM), lambda bi, hi: (bi, 0, hi)),
        out_shape=jax.ShapeDtypeStruct((b, t, n_heads * HEAD_DIM), BF16),
        scratch_shapes=[pltpu.VMEM((t, HEAD_DIM), BF16), pltpu.VMEM((t, HEAD_DIM), BF16),
                        pltpu.VMEM((n_dr + 1, GRID_W, 128), F32),
                        pltpu.VMEM((3, NA_GROUP * GRID_W, NA_SLAB_ROWS * GRID_W), F32)],
        compiler_params=_params("parallel", "parallel"),
        name="neighbourhood_attention",
    )(proj, proj, proj, rpb_pad, cmask)


def _swiglu_ffn(u, w_gate, w_up, w_down):
    act = gateup(u, w_gate, w_up)
    return matmul(act, w_down.astype(BF16), BF16, 512, 1024, "ffn_down", rhs_resident=True)


def kernel(x, ffn1_norm_pre, ffn1_w_gate, ffn1_w_up, ffn1_w_down, ffn1_norm_post, mix_norm_pre, w_in, hgrn_lb_logits, hgrn_head_norm, na_rpb, w_out, mix_norm_post, ffn2_norm_pre, ffn2_w_gate, ffn2_w_up, ffn2_w_down, ffn2_norm_post):
    b, t, d = x.shape
    depth = w_in.shape[0]
    hg_width = hgrn_lb_logits.shape[-1]
    hg_heads = hg_width // HEAD_DIM
    na_heads = na_rpb.shape[1]
    h = x.reshape(b * t, d)
    u = rmsnorm_cast(h, ffn1_norm_pre[0])
    for layer in range(depth):
        ff = _swiglu_ffn(u, ffn1_w_gate[layer], ffn1_w_up[layer], ffn1_w_down[layer])
        h, u = resnorm(h, ff, ffn1_norm_post[layer], FFN_RES_WEIGHT, mix_norm_pre[layer])

        proj = matmul(u, w_in[layer], F32, 2048, 512, "in_proj").reshape(b, t, -1)
        lb = jnp.cumsum(jax.nn.softmax(hgrn_lb_logits.astype(F32), axis=0), axis=0)[layer]
        o_hg = hgrn2(proj, lb, hgrn_head_norm[layer], hg_heads)
        o_na = neighbourhood_attention(proj, na_rpb[layer], na_heads, 5 * hg_heads)
        mixed = matmul_concat(o_hg.reshape(b * t, -1), o_na.reshape(b * t, -1), w_out[layer].astype(BF16),
                              BF16, 1024, 512, "out_proj")
        h, u = resnorm(h, mixed, mix_norm_post[layer], 1.0, ffn2_norm_pre[layer])

        ff = _swiglu_ffn(u, ffn2_w_gate[layer], ffn2_w_up[layer], ffn2_w_down[layer])
        if layer + 1 < depth:
            h, u = resnorm(h, ff, ffn2_norm_post[layer], FFN_RES_WEIGHT, ffn1_norm_pre[layer + 1])
        else:
            h = resnorm(h, ff, ffn2_norm_post[layer], FFN_RES_WEIGHT)
    return h.reshape(b, t, d)
```

```python
import functools
import math

import numpy as np
import jax
import jax.numpy as jnp
from jax import lax
from jax.experimental import pallas as pl
from jax.experimental.pallas import tpu as pltpu

EPS = 1e-6
HEAD_DIM = 128
GRID_W = 64
NA_WIN_ROWS = 8
NA_WIN_COLS = 16
HG_CHUNK = 128
HG_TILE_LEVEL = 3
HG_CHUNK_UNROLL = 8
NA_GROUP = 4
NA_SLAB_ROWS = NA_WIN_ROWS + NA_GROUP
NA_GROUP_UNROLL = 4
NA_INTERLEAVE = 4
FFN_RES_WEIGHT = 0.5
LOG2E = 1.0 / math.log(2.0)
NEG_BIG = -1e30
V7X_VMEM_LIMIT_BYTES = 56 * 1024 * 1024

BF16 = jnp.bfloat16
F32 = jnp.float32


def _params(*sem):
    return pltpu.CompilerParams(dimension_semantics=sem, vmem_limit_bytes=V7X_VMEM_LIMIT_BYTES)


def _dot(a, b):
    return jnp.dot(a, b, preferred_element_type=F32)


def _dot_nt(a, b):
    return lax.dot_general(a, b, (((1,), (1,)), ((), ())), preferred_element_type=F32)


def _dot_tn(a, b):
    return lax.dot_general(a, b, (((0,), (0,)), ((), ())), preferred_element_type=F32)


def _sigmoid(x):
    return 1.0 / (1.0 + jnp.exp2(x * -LOG2E))


def _silu(x):
    return x * _sigmoid(x)


def _tile(n, pref, mult):
    t = (min(pref, n) // mult) * mult
    while t >= mult:
        if n % t == 0:
            return t
        t -= mult
    return n


def _rmsnorm_cast_kernel(x_ref, g_ref, o_ref):
    x = x_ref[...]
    inv = lax.rsqrt(jnp.mean(x * x, axis=-1, keepdims=True) + EPS)
    o_ref[...] = (x * inv * g_ref[...]).astype(o_ref.dtype)


def rmsnorm_cast(x, gain):
    m, d = x.shape
    tm = _tile(m, 256, 8)
    return pl.pallas_call(
        _rmsnorm_cast_kernel,
        grid=(m // tm,),
        in_specs=[pl.BlockSpec((tm, d), lambda i: (i, 0)), pl.BlockSpec((1, d), lambda i: (0, 0))],
        out_specs=pl.BlockSpec((tm, d), lambda i: (i, 0)),
        out_shape=jax.ShapeDtypeStruct((m, d), BF16),
        compiler_params=_params("parallel"),
        name="rmsnorm_cast",
    )(x, gain.reshape(1, d))


def _resnorm_kernel(h_ref, y_ref, gp_ref, gn_ref, hn_ref, u_ref, *, weight):
    y = y_ref[...].astype(F32)
    inv = lax.rsqrt(jnp.mean(y * y, axis=-1, keepdims=True) + EPS)
    h = h_ref[...] + weight * (y * inv * gp_ref[...])
    hn_ref[...] = h
    inv2 = lax.rsqrt(jnp.mean(h * h, axis=-1, keepdims=True) + EPS)
    u_ref[...] = (h * inv2 * gn_ref[...]).astype(u_ref.dtype)


def _resnorm_last_kernel(h_ref, y_ref, gp_ref, hn_ref, *, weight):
    y = y_ref[...].astype(F32)
    inv = lax.rsqrt(jnp.mean(y * y, axis=-1, keepdims=True) + EPS)
    hn_ref[...] = h_ref[...] + weight * (y * inv * gp_ref[...])


def resnorm(h, y, gain_post, weight, gain_next=None):
    m, d = h.shape
    tm = _tile(m, 256, 8)
    row = pl.BlockSpec((tm, d), lambda i: (i, 0))
    vec = pl.BlockSpec((1, d), lambda i: (0, 0))
    if gain_next is None:
        return pl.pallas_call(
            functools.partial(_resnorm_last_kernel, weight=weight),
            grid=(m // tm,),
            in_specs=[row, row, vec],
            out_specs=row,
            out_shape=jax.ShapeDtypeStruct((m, d), F32),
            compiler_params=_params("parallel"),
            name="resnorm_last",
        )(h, y, gain_post.reshape(1, d))
    return pl.pallas_call(
        functools.partial(_resnorm_kernel, weight=weight),
        grid=(m // tm,),
        in_specs=[row, row, vec, vec],
        out_specs=[row, row],
        out_shape=[jax.ShapeDtypeStruct((m, d), F32), jax.ShapeDtypeStruct((m, d), BF16)],
        compiler_params=_params("parallel"),
        name="resnorm",
    )(h, y, gain_post.reshape(1, d), gain_next.reshape(1, d))


def _gateup_kernel(x_ref, wg_ref, wu_ref, o_ref):
    x = x_ref[...]
    g = _dot(x, wg_ref[...].astype(BF16))
    u = _dot(x, wu_ref[...].astype(BF16))
    o_ref[...] = (_silu(g) * u).astype(o_ref.dtype)


def gateup(x, wg, wu):
    m, k = x.shape
    n = wg.shape[1]
    tm = _tile(m, 1024, 8)
    tn = _tile(n, 256, 128)
    return pl.pallas_call(
        _gateup_kernel,
        grid=(m // tm, n // tn),
        in_specs=[pl.BlockSpec((tm, k), lambda i, j: (i, 0)),
                  pl.BlockSpec((k, tn), lambda i, j: (0, j)),
                  pl.BlockSpec((k, tn), lambda i, j: (0, j))],
        out_specs=pl.BlockSpec((tm, tn), lambda i, j: (i, j)),
        out_shape=jax.ShapeDtypeStruct((m, n), BF16),
        compiler_params=_params("parallel", "arbitrary"),
        name="ffn_gateup",
    )(x, wg, wu)


def _mm_kernel(a_ref, b_ref, o_ref):
    res = _dot(a_ref[...], b_ref[...].astype(BF16)).astype(o_ref.dtype)
    if len(o_ref.shape) == 3:
        for c in range(o_ref.shape[0]):
            o_ref[c] = res[:, c * HEAD_DIM:(c + 1) * HEAD_DIM]
    else:
        o_ref[...] = res


def matmul(a, b, out_dtype, tm_pref, tn_pref, name, rhs_resident=False, head_major=False):
    m, k = a.shape
    n = b.shape[1]
    tm = _tile(m, tm_pref, 8)
    tn = _tile(n, tn_pref, 128)
    once = pl.Buffered(1)
    out_shape = jax.ShapeDtypeStruct((m, n), out_dtype)
    if head_major:
        assert not rhs_resident
        grid = (m // tm, n // tn)
        a_spec = pl.BlockSpec((tm, k), lambda i, j: (i, 0), pipeline_mode=once)
        b_spec = pl.BlockSpec((k, tn), lambda i, j: (0, j))
        o_spec = pl.BlockSpec((tn // HEAD_DIM, tm, HEAD_DIM), lambda i, j: (j, i, 0))
        out_shape = jax.ShapeDtypeStruct((n // HEAD_DIM, m, HEAD_DIM), out_dtype)
    elif rhs_resident:
        grid = (n // tn, m // tm)
        a_spec = pl.BlockSpec((tm, k), lambda j, i: (i, 0))
        b_spec = pl.BlockSpec((k, tn), lambda j, i: (0, j), pipeline_mode=once)
        o_spec = pl.BlockSpec((tm, tn), lambda j, i: (i, j))
    else:
        grid = (m // tm, n // tn)
        a_spec = pl.BlockSpec((tm, k), lambda i, j: (i, 0), pipeline_mode=once)
        b_spec = pl.BlockSpec((k, tn), lambda i, j: (0, j))
        o_spec = pl.BlockSpec((tm, tn), lambda i, j: (i, j))
    return pl.pallas_call(
        _mm_kernel,
        grid=grid,
        in_specs=[a_spec, b_spec],
        out_specs=o_spec,
        out_shape=out_shape,
        compiler_params=_params("parallel", "arbitrary"),
        name=name,
    )(a, b)


def _mm2_kernel(a1_ref, a2_ref, b_ref, o_ref):
    k1 = a1_ref.shape[1]
    b = b_ref[...].astype(BF16)
    o_ref[...] = (_dot(a1_ref[...], b[:k1, :]) + _dot(a2_ref[...], b[k1:, :])).astype(o_ref.dtype)


def matmul_concat(a1, a2, b, out_dtype, tm_pref, tn_pref, name):
    m, k1 = a1.shape
    k2 = a2.shape[1]
    n = b.shape[1]
    tm = _tile(m, tm_pref, 8)
    tn = _tile(n, tn_pref, 128)
    return pl.pallas_call(
        _mm2_kernel,
        grid=(m // tm, n // tn),
        in_specs=[pl.BlockSpec((tm, k1), lambda i, j: (i, 0)),
                  pl.BlockSpec((tm, k2), lambda i, j: (i, 0)),
                  pl.BlockSpec((k1 + k2, tn), lambda i, j: (0, j))],
        out_specs=pl.BlockSpec((tm, tn), lambda i, j: (i, j)),
        out_shape=jax.ShapeDtypeStruct((m, n), out_dtype),
        compiler_params=_params("parallel", "arbitrary"),
        name=name,
    )(a1, a2, b)


def _hgrn_consts(chunk, reverse):
    nl = int(math.log2(chunk))
    assert 1 << nl == chunk and nl > HG_TILE_LEVEL
    t = np.arange(chunk)
    u = t[None, :]
    level_sums = np.zeros((nl, chunk, chunk), np.float32)
    qside = np.zeros((nl, chunk), bool)
    amask = np.zeros((nl + 1, chunk, chunk), np.float32)
    for l in range(nl):
        half = 1 << l
        blk = t // (2 * half)
        mid = blk * 2 * half + half
        upper = t >= mid
        level_sums[l] = np.where(upper[:, None],
                                 (u >= mid[:, None]) & (u <= t[:, None]),
                                 (u > t[:, None]) & (u <= mid[:, None] - 1))
        qside[l] = upper
        amask[l] = (blk[:, None] == blk[None, :]) & upper[:, None] & ~upper[None, :]
    amask[nl] = np.eye(chunk)
    sums = np.stack([level_sums[l] for l in range(1, HG_TILE_LEVEL)] + [(u <= t[:, None]).astype(np.float32)])
    if reverse:
        sums = sums[:, ::-1, ::-1]
        qside = qside[:, ::-1]
        amask = amask[:, ::-1, ::-1]
    sums = np.ascontiguousarray(sums).reshape(len(sums) * chunk, chunk)
    sums = np.concatenate([sums, sums], axis=1)
    side = np.where(qside[:HG_TILE_LEVEL], 1.0, -1.0)
    side = np.ascontiguousarray(np.broadcast_to(side[:, :, None], (HG_TILE_LEVEL, chunk, HEAD_DIM)))
    am_low = np.concatenate([amask[:HG_TILE_LEVEL], amask[nl:]])
    am_high = np.stack([amask[l][qside[l]] for l in range(HG_TILE_LEVEL, nl)])
    return (jnp.asarray(sums, BF16), jnp.asarray(side, F32),
            jnp.asarray(np.ascontiguousarray(am_low), F32), jnp.asarray(np.ascontiguousarray(am_high), F32))


def _run_interleaved(stage_generators):
    live = list(stage_generators)
    while live:
        for gen in list(live):
            try:
                next(gen)
            except StopIteration:
                live.remove(gen)


def _hgrn_kernel(hq_ref, ff_ref, fb_ref, hi_ref, hg_ref, lb_ref, hn_ref, *rest, chunk, nl):
    consts_f, consts_b = rest[:4], rest[4:8]
    out_ref, of_ref, ob_ref, stf_ref, stb_ref = rest[8:]
    seq = hq_ref.shape[0]
    nchunks = seq // chunk
    stf_ref[...] = jnp.zeros_like(stf_ref)
    stb_ref[...] = jnp.zeros_like(stb_ref)

    def chunk_step(base, direction, f_ref, consts, st_ref, o_ref):
        sums_ref, side_ref, am_low_ref, am_high_ref = consts
        reverse = bool(direction)
        rows = pl.ds(pl.multiple_of(base, chunk), chunk)
        q = _silu(hq_ref[rows, :])
        lb = lb_ref[direction:direction + 1, :]
        f = lb + (1.0 - lb) * _sigmoid(f_ref[rows, :])
        k = 1.0 - f
        g = jnp.log(f) * LOG2E
        v = hi_ref[rows, :].astype(BF16)
        g1 = g.astype(BF16)
        g2 = (g - g1.astype(F32)).astype(BF16)
        z = _dot(sums_ref[...], jnp.concatenate([g1, g2], axis=0))
        yield
        cum = z[(HG_TILE_LEVEL - 1) * chunk:]

        a = am_low_ref[HG_TILE_LEVEL] * _dot_nt(q.astype(BF16), k.astype(BF16))
        for l in range(HG_TILE_LEVEL):
            qside = side_ref[l] > 0.0
            if l == 0:
                x = jnp.where(qside, q * f, k)
            else:
                x = jnp.where(qside, q, k) * jnp.exp2(z[(l - 1) * chunk:l * chunk])
            x = x.astype(BF16)
            a = a + am_low_ref[l] * _dot_nt(x, x)
            yield

        a_tiles = [a[8 * p:8 * p + 8] for p in range(chunk // 8)]
        for l in range(HG_TILE_LEVEL, nl):
            half = 1 << l
            x_parts, q_parts, q_tiles = [], [], []
            for start in range(0, chunk, 2 * half):
                mid = start + half
                ref = cum[mid:mid + 1] if reverse else cum[mid - 1:mid]
                for lo, is_q in ((start, reverse), (mid, not reverse)):
                    if is_q:
                        part = q[lo:lo + half] * jnp.exp2(cum[lo:lo + half] - ref)
                        q_parts.append(part)
                        q_tiles.extend(range(lo // 8, (lo + half) // 8))
                    else:
                        part = k[lo:lo + half] * jnp.exp2(ref - cum[lo:lo + half])
                    x_parts.append(part)
            x = jnp.concatenate(x_parts, axis=0).astype(BF16)
            xq = (q_parts[0] if len(q_parts) == 1 else jnp.concatenate(q_parts, axis=0)).astype(BF16)
            scores = am_high_ref[l - HG_TILE_LEVEL] * _dot_nt(xq, x)
            for i, p in enumerate(q_tiles):
                a_tiles[p] = a_tiles[p] + scores[8 * i:8 * i + 8]
            yield
        a = jnp.concatenate(a_tiles, axis=0)

        last = 0 if reverse else chunk - 1
        cum_last = cum[last:last + 1, :]
        qd = (q * jnp.exp2(cum)).astype(BF16)
        kd = (k * jnp.exp2(cum_last - cum)).astype(BF16)
        yield
        st = st_ref[...]
        o_ref[rows, :] = _dot(a.astype(BF16), v) + _dot_nt(qd, st.astype(BF16))
        st_ref[...] = st * jnp.exp2(cum_last) + _dot_tn(v, kd)

    def finish(base):
        rows = pl.ds(pl.multiple_of(base, chunk), chunk)
        o = of_ref[rows, :] + ob_ref[rows, :]
        inv = lax.rsqrt(jnp.mean(o * o, axis=-1, keepdims=True) + EPS)
        out_ref[rows, :] = (o * inv * hn_ref[...] * _silu(hg_ref[rows, :])).astype(out_ref.dtype)

    def body(n, carry, both_done):
        fwd_base, bwd_base = n * chunk, (nchunks - 1 - n) * chunk
        _run_interleaved([chunk_step(fwd_base, 0, ff_ref, consts_f, stf_ref, of_ref),
                          chunk_step(bwd_base, 1, fb_ref, consts_b, stb_ref, ob_ref)])
        if both_done:
            finish(fwd_base)
            finish(bwd_base)
        return carry

    crossed = nchunks // 2
    lax.fori_loop(0, crossed, functools.partial(body, both_done=False), 0,
                  unroll=max(1, min(HG_CHUNK_UNROLL, crossed)))
    lax.fori_loop(crossed, nchunks, functools.partial(body, both_done=True), 0,
                  unroll=min(HG_CHUNK_UNROLL, nchunks - crossed))


def hgrn2(proj, lb, head_norm, n_heads, t):
    b = proj.shape[1] // t
    chunk = min(HG_CHUNK, t)
    nl = int(math.log2(chunk))
    consts_f = _hgrn_consts(chunk, False)
    consts_b = _hgrn_consts(chunk, True)

    def col(group):
        return pl.BlockSpec((None, t, HEAD_DIM), lambda bi, hi, g=group: (g * n_heads + hi, bi, 0))

    def const(arr):
        return pl.BlockSpec(arr.shape, lambda bi, hi, nd=arr.ndim: (0,) * nd)

    return pl.pallas_call(
        functools.partial(_hgrn_kernel, chunk=chunk, nl=nl),
        grid=(b, n_heads),
        in_specs=[col(0), col(1), col(2), col(3), col(4),
                  pl.BlockSpec((2, HEAD_DIM), lambda bi, hi: (0, hi)),
                  pl.BlockSpec((1, HEAD_DIM), lambda bi, hi: (0, 0)),
                  *[const(c) for c in consts_f], *[const(c) for c in consts_b]],
        out_specs=pl.BlockSpec((None, t, HEAD_DIM), lambda bi, hi: (bi, 0, hi)),
        out_shape=jax.ShapeDtypeStruct((b, t, n_heads * HEAD_DIM), BF16),
        scratch_shapes=[pltpu.VMEM((t, HEAD_DIM), F32), pltpu.VMEM((t, HEAD_DIM), F32),
                        pltpu.VMEM((HEAD_DIM, HEAD_DIM), F32), pltpu.VMEM((HEAD_DIM, HEAD_DIM), F32)],
        compiler_params=_params("parallel", "parallel"),
        name="hgrn2",
    )(proj, proj, proj, proj, proj, lb, head_norm.reshape(1, HEAD_DIM), *consts_f, *consts_b)


def _na_col_mask():
    col = np.arange(GRID_W)
    col_start = np.clip(col - NA_WIN_COLS // 2, 0, GRID_W - NA_WIN_COLS)
    inside = (col[None, :] >= col_start[:, None]) & (col[None, :] < col_start[:, None] + NA_WIN_COLS)
    return jnp.asarray(np.tile(np.where(inside, 0.0, NEG_BIG), (1, 128 // GRID_W)), F32)


def _na_build_bias(rpb_ref, cmask_ref, tab_ref, bias_ref):
    lane = lax.broadcasted_iota(jnp.int32, (GRID_W, 128), 1)
    first_half = lane < GRID_W
    back = NA_WIN_COLS - 1
    n_dr = 2 * NA_WIN_ROWS - 1
    for dr in range(n_dr):
        row = jnp.broadcast_to(rpb_ref[dr:dr + 1, :] * LOG2E, (GRID_W, 128))
        lo = pltpu.roll(row, 128 - back, 1, stride=1, stride_axis=0)
        hi = pltpu.roll(row, GRID_W - back, 1, stride=1, stride_axis=0)
        tab_ref[dr] = jnp.where(first_half, lo, hi) + cmask_ref[...]
    tab_ref[n_dr] = jnp.full((GRID_W, 128), NEG_BIG, F32)
    for variant, delta in enumerate((0, -NA_GROUP, -2 * NA_GROUP)):
        for i in range(NA_GROUP):
            win_lo = (0, i, NA_GROUP)[variant]

            def tile(j):
                inside = win_lo <= j < win_lo + NA_WIN_ROWS
                return tab_ref[delta + j - i + NA_WIN_ROWS - 1 if inside else n_dr]

            for pair in range(NA_SLAB_ROWS // 2):
                bias_ref[variant, i * GRID_W:(i + 1) * GRID_W, pair * 128:(pair + 1) * 128] = jnp.where(
                    first_half, tile(2 * pair), tile(2 * pair + 1))


def _na_kernel(q_ref, k_ref, v_ref, rpb_ref, cmask_ref, out_ref, kb_ref, vb_ref, tab_ref, bias_ref, *, scale):
    seq = q_ref.shape[0]
    n_groups = seq // GRID_W // NA_GROUP
    group = NA_GROUP * GRID_W
    slab = NA_SLAB_ROWS * GRID_W
    kb_ref[...] = k_ref[...].astype(BF16)
    vb_ref[...] = v_ref[...].astype(BF16)
    _na_build_bias(rpb_ref, cmask_ref, tab_ref, bias_ref)

    def group_step(gi):
        slab_start = jnp.clip(gi - 1, 0, n_groups - NA_SLAB_ROWS // NA_GROUP) * group
        variant = jnp.where(gi == 0, 0, jnp.where(gi == n_groups - 1, 2, 1))
        qrows = pl.ds(pl.multiple_of(gi * group, group), group)
        krows = pl.ds(pl.multiple_of(slab_start, group), slab)
        q = (q_ref[qrows, :] * scale).astype(BF16)
        s = _dot_nt(q, kb_ref[krows, :]) + bias_ref[variant]
        yield
        m = jnp.max(s, axis=-1, keepdims=True)
        p = jnp.exp2(s - m)
        yield
        denom = jnp.sum(p, axis=-1, keepdims=True)
        o = _dot(p.astype(BF16), vb_ref[krows, :]) / denom
        yield
        out_ref[qrows, :] = o.astype(out_ref.dtype)

    def body(pi, carry):
        _run_interleaved([group_step(pi * NA_INTERLEAVE + c) for c in range(NA_INTERLEAVE)])
        return carry

    assert n_groups % NA_INTERLEAVE == 0
    lax.fori_loop(0, n_groups // NA_INTERLEAVE, body, 0, unroll=max(1, NA_GROUP_UNROLL // NA_INTERLEAVE))


def neighbourhood_attention(proj, rpb, n_heads, col0, t):
    b = proj.shape[1] // t
    n_rows = t // GRID_W
    assert NA_GROUP == NA_WIN_ROWS // 2 and NA_SLAB_ROWS % NA_GROUP == 0 and (NA_SLAB_ROWS * GRID_W) % 128 == 0
    assert t % GRID_W == 0 and n_rows % NA_GROUP == 0 and n_rows >= NA_SLAB_ROWS + NA_GROUP
    n_dr, n_dc = rpb.shape[1:]
    assert (n_dr, n_dc) == (2 * NA_WIN_ROWS - 1, 2 * NA_WIN_COLS - 1)
    rpb_pad = jnp.pad(rpb.astype(F32), ((0, 0), (0, 1), (0, 128 - n_dc)))
    cmask = _na_col_mask()

    def col(group):
        return pl.BlockSpec((None, t, HEAD_DIM), lambda bi, hi, g=group: (col0 + g * n_heads + hi, bi, 0))

    return pl.pallas_call(
        functools.partial(_na_kernel, scale=HEAD_DIM ** -0.5 * LOG2E),
        grid=(b, n_heads),
        in_specs=[col(0), col(1), col(2),
                  pl.BlockSpec((None, n_dr + 1, 128), lambda bi, hi: (hi, 0, 0)),
                  pl.BlockSpec(cmask.shape, lambda bi, hi: (0, 0))],
        out_specs=pl.BlockSpec((None, t, HEAD_DIM), lambda bi, hi: (bi, 0, hi)),
        out_shape=jax.ShapeDtypeStruct((b, t, n_heads * HEAD_DIM), BF16),
        scratch_shapes=[pltpu.VMEM((t, HEAD_DIM), BF16), pltpu.VMEM((t, HEAD_DIM), BF16),
                        pltpu.VMEM((n_dr + 1, GRID_W, 128), F32),
                        pltpu.VMEM((3, NA_GROUP * GRID_W, NA_SLAB_ROWS * GRID_W), F32)],
        compiler_params=_params("parallel", "parallel"),
        name="neighbourhood_attention",
    )(proj, proj, proj, rpb_pad, cmask)


def _swiglu_ffn(u, w_gate, w_up, w_down):
    act = gateup(u, w_gate, w_up)
    return matmul(act, w_down.astype(BF16), BF16, 512, 1024, "ffn_down", rhs_resident=True)


def kernel(x, ffn1_norm_pre, ffn1_w_gate, ffn1_w_up, ffn1_w_down, ffn1_norm_post, mix_norm_pre, w_in, hgrn_lb_logits, hgrn_head_norm, na_rpb, w_out, mix_norm_post, ffn2_norm_pre, ffn2_w_gate, ffn2_w_up, ffn2_w_down, ffn2_norm_post):
    b, t, d = x.shape
    depth = w_in.shape[0]
    hg_width = hgrn_lb_logits.shape[-1]
    hg_heads = hg_width // HEAD_DIM
    na_heads = na_rpb.shape[1]
    h = x.reshape(b * t, d)
    u = rmsnorm_cast(h, ffn1_norm_pre[0])
    for layer in range(depth):
        ff = _swiglu_ffn(u, ffn1_w_gate[layer], ffn1_w_up[layer], ffn1_w_down[layer])
        h, u = resnorm(h, ff, ffn1_norm_post[layer], FFN_RES_WEIGHT, mix_norm_pre[layer])

        proj = matmul(u, w_in[layer], F32, 2048, 512, "in_proj", head_major=True)
        lb = jnp.cumsum(jax.nn.softmax(hgrn_lb_logits.astype(F32), axis=0), axis=0)[layer]
        o_hg = hgrn2(proj, lb, hgrn_head_norm[layer], hg_heads, t)
        o_na = neighbourhood_attention(proj, na_rpb[layer], na_heads, 5 * hg_heads, t)
        mixed = matmul_concat(o_hg.reshape(b * t, -1), o_na.reshape(b * t, -1), w_out[layer],
                              BF16, 1024, 512, "out_proj")
        h, u = resnorm(h, mixed, mix_norm_post[layer], 1.0, ffn2_norm_pre[layer])

        ff = _swiglu_ffn(u, ffn2_w_gate[layer], ffn2_w_up[layer], ffn2_w_down[layer])
        if layer + 1 < depth:
            h, u = resnorm(h, ff, ffn2_norm_post[layer], FFN_RES_WEIGHT, ffn1_norm_pre[layer + 1])
        else:
            h = resnorm(h, ff, ffn2_norm_post[layer], FFN_RES_WEIGHT)
    return h.reshape(b, t, d)
```

```python
import functools
import math

import numpy as np
import jax
import jax.numpy as jnp
from jax import lax
from jax.experimental import pallas as pl
from jax.experimental.pallas import tpu as pltpu

EPS = 1e-6
HEAD_DIM = 128
GRID_W = 64
NA_WIN_ROWS = 8
NA_WIN_COLS = 16
HG_CHUNK = 128
HG_TILE_LEVEL = 3
HG_CHUNK_UNROLL = 16
NA_GROUP = 4
NA_SLAB_ROWS = NA_WIN_ROWS + NA_GROUP
NA_GROUP_UNROLL = 16
NA_INTERLEAVE = 4
FFN_RES_WEIGHT = 0.5
LOG2E = 1.0 / math.log(2.0)
NEG_BIG = -1e30
V7X_VMEM_LIMIT_BYTES = 56 * 1024 * 1024

BF16 = jnp.bfloat16
F32 = jnp.float32


def _params(*sem):
    return pltpu.CompilerParams(dimension_semantics=sem, vmem_limit_bytes=V7X_VMEM_LIMIT_BYTES)


def _dot(a, b):
    return jnp.dot(a, b, preferred_element_type=F32)


def _dot_nt(a, b):
    return lax.dot_general(a, b, (((1,), (1,)), ((), ())), preferred_element_type=F32)


def _dot_tn(a, b):
    return lax.dot_general(a, b, (((0,), (0,)), ((), ())), preferred_element_type=F32)


def _sigmoid(x):
    return 1.0 / (1.0 + jnp.exp2(x * -LOG2E))


def _silu(x):
    return x * _sigmoid(x)


def _tile(n, pref, mult):
    t = (min(pref, n) // mult) * mult
    while t >= mult:
        if n % t == 0:
            return t
        t -= mult
    return n


def _rmsnorm_cast_kernel(x_ref, g_ref, o_ref):
    x = x_ref[...]
    inv = lax.rsqrt(jnp.mean(x * x, axis=-1, keepdims=True) + EPS)
    o_ref[...] = (x * inv * g_ref[...]).astype(o_ref.dtype)


def rmsnorm_cast(x, gain):
    m, d = x.shape
    tm = _tile(m, 256, 8)
    return pl.pallas_call(
        _rmsnorm_cast_kernel,
        grid=(m // tm,),
        in_specs=[pl.BlockSpec((tm, d), lambda i: (i, 0)), pl.BlockSpec((1, d), lambda i: (0, 0))],
        out_specs=pl.BlockSpec((tm, d), lambda i: (i, 0)),
        out_shape=jax.ShapeDtypeStruct((m, d), BF16),
        compiler_params=_params("parallel"),
        name="rmsnorm_cast",
    )(x, gain.reshape(1, d))


def _resnorm_kernel(h_ref, y_ref, gp_ref, gn_ref, hn_ref, u_ref, *, weight):
    y = y_ref[...].astype(F32)
    inv = lax.rsqrt(jnp.mean(y * y, axis=-1, keepdims=True) + EPS)
    h = h_ref[...] + weight * (y * inv * gp_ref[...])
    hn_ref[...] = h
    inv2 = lax.rsqrt(jnp.mean(h * h, axis=-1, keepdims=True) + EPS)
    u_ref[...] = (h * inv2 * gn_ref[...]).astype(u_ref.dtype)


def _resnorm_last_kernel(h_ref, y_ref, gp_ref, hn_ref, *, weight):
    y = y_ref[...].astype(F32)
    inv = lax.rsqrt(jnp.mean(y * y, axis=-1, keepdims=True) + EPS)
    hn_ref[...] = h_ref[...] + weight * (y * inv * gp_ref[...])


def resnorm(h, y, gain_post, weight, gain_next=None):
    m, d = h.shape
    tm = _tile(m, 256, 8)
    row = pl.BlockSpec((tm, d), lambda i: (i, 0))
    vec = pl.BlockSpec((1, d), lambda i: (0, 0))
    if gain_next is None:
        return pl.pallas_call(
            functools.partial(_resnorm_last_kernel, weight=weight),
            grid=(m // tm,),
            in_specs=[row, row, vec],
            out_specs=row,
            out_shape=jax.ShapeDtypeStruct((m, d), F32),
            compiler_params=_params("parallel"),
            name="resnorm_last",
        )(h, y, gain_post.reshape(1, d))
    return pl.pallas_call(
        functools.partial(_resnorm_kernel, weight=weight),
        grid=(m // tm,),
        in_specs=[row, row, vec, vec],
        out_specs=[row, row],
        out_shape=[jax.ShapeDtypeStruct((m, d), F32), jax.ShapeDtypeStruct((m, d), BF16)],
        compiler_params=_params("parallel"),
        name="resnorm",
    )(h, y, gain_post.reshape(1, d), gain_next.reshape(1, d))


def _gateup_kernel(x_ref, wg_ref, wu_ref, wd_ref, o_ref, wd16_ref):
    x = x_ref[...]
    g = _dot(x, wg_ref[...].astype(BF16))
    u = _dot(x, wu_ref[...].astype(BF16))
    o_ref[...] = (_silu(g) * u).astype(o_ref.dtype)
    wd16_ref[...] = wd_ref[...].astype(BF16)


def gateup(x, wg, wu, wd):
    m, k = x.shape
    n = wg.shape[1]
    tm = _tile(m, 1024, 8)
    tn = _tile(n, 256, 128)
    steps_per_row = n // tn
    wd_rows = wd.shape[0] // ((m // tm) * steps_per_row)
    assert wd_rows * (m // tm) * steps_per_row == wd.shape[0] and wd_rows % 16 == 0
    wd_spec = pl.BlockSpec((wd_rows, wd.shape[1]), lambda i, j: (i * steps_per_row + j, 0))
    return pl.pallas_call(
        _gateup_kernel,
        grid=(m // tm, steps_per_row),
        in_specs=[pl.BlockSpec((tm, k), lambda i, j: (i, 0)),
                  pl.BlockSpec((k, tn), lambda i, j: (0, j)),
                  pl.BlockSpec((k, tn), lambda i, j: (0, j)),
                  wd_spec],
        out_specs=[pl.BlockSpec((tm, tn), lambda i, j: (i, j)), wd_spec],
        out_shape=[jax.ShapeDtypeStruct((m, n), BF16), jax.ShapeDtypeStruct(wd.shape, BF16)],
        compiler_params=_params("parallel", "arbitrary"),
        name="ffn_gateup",
    )(x, wg, wu, wd)


def _mm_kernel(a_ref, b_ref, o_ref):
    res = _dot(a_ref[...], b_ref[...].astype(BF16)).astype(o_ref.dtype)
    if len(o_ref.shape) == 3:
        for c in range(o_ref.shape[0]):
            o_ref[c] = res[:, c * HEAD_DIM:(c + 1) * HEAD_DIM]
    else:
        o_ref[...] = res


def matmul(a, b, out_dtype, tm_pref, tn_pref, name, rhs_resident=False, head_major=False):
    m, k = a.shape
    n = b.shape[1]
    tm = _tile(m, tm_pref, 8)
    tn = _tile(n, tn_pref, 128)
    once = pl.Buffered(1)
    out_shape = jax.ShapeDtypeStruct((m, n), out_dtype)
    if head_major:
        assert not rhs_resident
        grid = (m // tm, n // tn)
        a_spec = pl.BlockSpec((tm, k), lambda i, j: (i, 0), pipeline_mode=once)
        b_spec = pl.BlockSpec((k, tn), lambda i, j: (0, j))
        o_spec = pl.BlockSpec((tn // HEAD_DIM, tm, HEAD_DIM), lambda i, j: (j, i, 0))
        out_shape = jax.ShapeDtypeStruct((n // HEAD_DIM, m, HEAD_DIM), out_dtype)
    elif rhs_resident:
        grid = (n // tn, m // tm)
        a_spec = pl.BlockSpec((tm, k), lambda j, i: (i, 0))
        b_spec = pl.BlockSpec((k, tn), lambda j, i: (0, j), pipeline_mode=once)
        o_spec = pl.BlockSpec((tm, tn), lambda j, i: (i, j))
    else:
        grid = (m // tm, n // tn)
        a_spec = pl.BlockSpec((tm, k), lambda i, j: (i, 0), pipeline_mode=once)
        b_spec = pl.BlockSpec((k, tn), lambda i, j: (0, j))
        o_spec = pl.BlockSpec((tm, tn), lambda i, j: (i, j))
    return pl.pallas_call(
        _mm_kernel,
        grid=grid,
        in_specs=[a_spec, b_spec],
        out_specs=o_spec,
        out_shape=out_shape,
        compiler_params=_params("parallel", "arbitrary"),
        name=name,
    )(a, b)


def _mm2_kernel(a1_ref, a2_ref, b_ref, o_ref):
    k1 = a1_ref.shape[1]
    b = b_ref[...].astype(BF16)
    o_ref[...] = (_dot(a1_ref[...], b[:k1, :]) + _dot(a2_ref[...], b[k1:, :])).astype(o_ref.dtype)


def matmul_concat(a1, a2, b, out_dtype, tm_pref, tn_pref, name):
    m, k1 = a1.shape
    k2 = a2.shape[1]
    n = b.shape[1]
    tm = _tile(m, tm_pref, 8)
    tn = _tile(n, tn_pref, 128)
    return pl.pallas_call(
        _mm2_kernel,
        grid=(m // tm, n // tn),
        in_specs=[pl.BlockSpec((tm, k1), lambda i, j: (i, 0)),
                  pl.BlockSpec((tm, k2), lambda i, j: (i, 0)),
                  pl.BlockSpec((k1 + k2, tn), lambda i, j: (0, j))],
        out_specs=pl.BlockSpec((tm, tn), lambda i, j: (i, j)),
        out_shape=jax.ShapeDtypeStruct((m, n), out_dtype),
        compiler_params=_params("parallel", "arbitrary"),
        name=name,
    )(a1, a2, b)


def _hgrn_consts(chunk, reverse):
    nl = int(math.log2(chunk))
    assert 1 << nl == chunk and nl > HG_TILE_LEVEL
    t = np.arange(chunk)
    u = t[None, :]
    level_sums = np.zeros((nl, chunk, chunk), np.float32)
    qside = np.zeros((nl, chunk), bool)
    amask = np.zeros((nl + 1, chunk, chunk), np.float32)
    for l in range(nl):
        half = 1 << l
        blk = t // (2 * half)
        mid = blk * 2 * half + half
        upper = t >= mid
        level_sums[l] = np.where(upper[:, None],
                                 (u >= mid[:, None]) & (u <= t[:, None]),
                                 (u > t[:, None]) & (u <= mid[:, None] - 1))
        qside[l] = upper
        amask[l] = (blk[:, None] == blk[None, :]) & upper[:, None] & ~upper[None, :]
    amask[nl] = np.eye(chunk)
    sums = np.stack([level_sums[l] for l in range(1, HG_TILE_LEVEL)] + [(u <= t[:, None]).astype(np.float32)])
    if reverse:
        sums = sums[:, ::-1, ::-1]
        qside = qside[:, ::-1]
        amask = amask[:, ::-1, ::-1]
    sums = np.ascontiguousarray(sums).reshape(len(sums) * chunk, chunk)
    sums = np.concatenate([sums, sums], axis=1)
    side = np.where(qside[:HG_TILE_LEVEL], 1.0, -1.0)
    side = np.ascontiguousarray(np.broadcast_to(side[:, :, None], (HG_TILE_LEVEL, chunk, HEAD_DIM)))
    am_low = np.concatenate([amask[:HG_TILE_LEVEL], amask[nl:]])
    am_high = np.stack([amask[l][qside[l]] for l in range(HG_TILE_LEVEL, nl)])
    return (jnp.asarray(sums, BF16), jnp.asarray(side, F32),
            jnp.asarray(np.ascontiguousarray(am_low), F32), jnp.asarray(np.ascontiguousarray(am_high), F32))


def _run_interleaved(stage_generators):
    live = list(stage_generators)
    while live:
        for gen in list(live):
            try:
                next(gen)
            except StopIteration:
                live.remove(gen)


def _hgrn_kernel(hq_ref, ff_ref, fb_ref, hi_ref, hg_ref, lb_ref, hn_ref, *rest, chunk, nl):
    consts_f, consts_b = rest[:4], rest[4:8]
    out_ref, of_ref, ob_ref, stf_ref, stb_ref = rest[8:]
    seq = hq_ref.shape[0]
    nchunks = seq // chunk
    stf_ref[...] = jnp.zeros_like(stf_ref)
    stb_ref[...] = jnp.zeros_like(stb_ref)

    def chunk_step(base, direction, f_ref, consts, st_ref, o_ref):
        sums_ref, side_ref, am_low_ref, am_high_ref = consts
        reverse = bool(direction)
        rows = pl.ds(pl.multiple_of(base, chunk), chunk)
        q = _silu(hq_ref[rows, :])
        lb = lb_ref[direction:direction + 1, :]
        f = lb + (1.0 - lb) * _sigmoid(f_ref[rows, :])
        k = 1.0 - f
        g = jnp.log(f) * LOG2E
        v = hi_ref[rows, :].astype(BF16)
        g1 = g.astype(BF16)
        g2 = (g - g1.astype(F32)).astype(BF16)
        z = _dot(sums_ref[...], jnp.concatenate([g1, g2], axis=0))
        yield
        cum = z[(HG_TILE_LEVEL - 1) * chunk:]

        a = am_low_ref[HG_TILE_LEVEL] * _dot_nt(q.astype(BF16), k.astype(BF16))
        for l in range(HG_TILE_LEVEL):
            qside = side_ref[l] > 0.0
            if l == 0:
                x = jnp.where(qside, q * f, k)
            else:
                x = jnp.where(qside, q, k) * jnp.exp2(z[(l - 1) * chunk:l * chunk])
            x = x.astype(BF16)
            a = a + am_low_ref[l] * _dot_nt(x, x)
            yield

        a_tiles = [a[8 * p:8 * p + 8] for p in range(chunk // 8)]
        for l in range(HG_TILE_LEVEL, nl):
            half = 1 << l
            x_parts, q_parts, q_tiles = [], [], []
            for start in range(0, chunk, 2 * half):
                mid = start + half
                ref = cum[mid:mid + 1] if reverse else cum[mid - 1:mid]
                for lo, is_q in ((start, reverse), (mid, not reverse)):
                    if is_q:
                        part = q[lo:lo + half] * jnp.exp2(cum[lo:lo + half] - ref)
                        q_parts.append(part)
                        q_tiles.extend(range(lo // 8, (lo + half) // 8))
                    else:
                        part = k[lo:lo + half] * jnp.exp2(ref - cum[lo:lo + half])
                    x_parts.append(part)
            x = jnp.concatenate(x_parts, axis=0).astype(BF16)
            xq = (q_parts[0] if len(q_parts) == 1 else jnp.concatenate(q_parts, axis=0)).astype(BF16)
            scores = am_high_ref[l - HG_TILE_LEVEL] * _dot_nt(xq, x)
            for i, p in enumerate(q_tiles):
                a_tiles[p] = a_tiles[p] + scores[8 * i:8 * i + 8]
            yield
        a = jnp.concatenate(a_tiles, axis=0)

        last = 0 if reverse else chunk - 1
        cum_last = cum[last:last + 1, :]
        qd = (q * jnp.exp2(cum)).astype(BF16)
        kd = (k * jnp.exp2(cum_last - cum)).astype(BF16)
        yield
        st = st_ref[...]
        o_ref[rows, :] = _dot(a.astype(BF16), v) + _dot_nt(qd, st.astype(BF16))
        st_ref[...] = st * jnp.exp2(cum_last) + _dot_tn(v, kd)

    def finish(base):
        rows = pl.ds(pl.multiple_of(base, chunk), chunk)
        o = of_ref[rows, :] + ob_ref[rows, :]
        inv = lax.rsqrt(jnp.mean(o * o, axis=-1, keepdims=True) + EPS)
        out_ref[rows, :] = (o * inv * hn_ref[...] * _silu(hg_ref[rows, :])).astype(out_ref.dtype)

    def body(n, carry, both_done):
        fwd_base, bwd_base = n * chunk, (nchunks - 1 - n) * chunk
        _run_interleaved([chunk_step(fwd_base, 0, ff_ref, consts_f, stf_ref, of_ref),
                          chunk_step(bwd_base, 1, fb_ref, consts_b, stb_ref, ob_ref)])
        if both_done:
            finish(fwd_base)
            finish(bwd_base)
        return carry

    crossed = nchunks // 2
    lax.fori_loop(0, crossed, functools.partial(body, both_done=False), 0,
                  unroll=max(1, min(HG_CHUNK_UNROLL, crossed)))
    lax.fori_loop(crossed, nchunks, functools.partial(body, both_done=True), 0,
                  unroll=min(HG_CHUNK_UNROLL, nchunks - crossed))


def hgrn2(proj, lb, head_norm, n_heads, t):
    b = proj.shape[1] // t
    chunk = min(HG_CHUNK, t)
    nl = int(math.log2(chunk))
    consts_f = _hgrn_consts(chunk, False)
    consts_b = _hgrn_consts(chunk, True)

    def col(group):
        return pl.BlockSpec((None, t, HEAD_DIM), lambda bi, hi, g=group: (g * n_heads + hi, bi, 0))

    def const(arr):
        return pl.BlockSpec(arr.shape, lambda bi, hi, nd=arr.ndim: (0,) * nd)

    return pl.pallas_call(
        functools.partial(_hgrn_kernel, chunk=chunk, nl=nl),
        grid=(b, n_heads),
        in_specs=[col(0), col(1), col(2), col(3), col(4),
                  pl.BlockSpec((2, HEAD_DIM), lambda bi, hi: (0, hi)),
                  pl.BlockSpec((1, HEAD_DIM), lambda bi, hi: (0, 0)),
                  *[const(c) for c in consts_f], *[const(c) for c in consts_b]],
        out_specs=pl.BlockSpec((None, t, HEAD_DIM), lambda bi, hi: (bi, 0, hi)),
        out_shape=jax.ShapeDtypeStruct((b, t, n_heads * HEAD_DIM), BF16),
        scratch_shapes=[pltpu.VMEM((t, HEAD_DIM), F32), pltpu.VMEM((t, HEAD_DIM), F32),
                        pltpu.VMEM((HEAD_DIM, HEAD_DIM), F32), pltpu.VMEM((HEAD_DIM, HEAD_DIM), F32)],
        compiler_params=_params("parallel", "parallel"),
        name="hgrn2",
    )(proj, proj, proj, proj, proj, lb, head_norm.reshape(1, HEAD_DIM), *consts_f, *consts_b)


def _na_col_mask():
    col = np.arange(GRID_W)
    col_start = np.clip(col - NA_WIN_COLS // 2, 0, GRID_W - NA_WIN_COLS)
    inside = (col[None, :] >= col_start[:, None]) & (col[None, :] < col_start[:, None] + NA_WIN_COLS)
    return jnp.asarray(np.tile(np.where(inside, 0.0, NEG_BIG), (1, 128 // GRID_W)), F32)


def _na_build_bias(rpb_ref, cmask_ref, tab_ref, bias_ref):
    lane = lax.broadcasted_iota(jnp.int32, (GRID_W, 128), 1)
    first_half = lane < GRID_W
    back = NA_WIN_COLS - 1
    n_dr = 2 * NA_WIN_ROWS - 1
    for dr in range(n_dr):
        row = jnp.broadcast_to(rpb_ref[dr:dr + 1, :] * LOG2E, (GRID_W, 128))
        lo = pltpu.roll(row, 128 - back, 1, stride=1, stride_axis=0)
        hi = pltpu.roll(row, GRID_W - back, 1, stride=1, stride_axis=0)
        tab_ref[dr] = jnp.where(first_half, lo, hi) + cmask_ref[...]
    tab_ref[n_dr] = jnp.full((GRID_W, 128), NEG_BIG, F32)
    for variant, delta in enumerate((0, -NA_GROUP, -2 * NA_GROUP)):
        for i in range(NA_GROUP):
            win_lo = (0, i, NA_GROUP)[variant]

            def tile(j):
                inside = win_lo <= j < win_lo + NA_WIN_ROWS
                return tab_ref[delta + j - i + NA_WIN_ROWS - 1 if inside else n_dr]

            for pair in range(NA_SLAB_ROWS // 2):
                bias_ref[variant, i * GRID_W:(i + 1) * GRID_W, pair * 128:(pair + 1) * 128] = jnp.where(
                    first_half, tile(2 * pair), tile(2 * pair + 1))


def _na_kernel(q_ref, k_ref, v_ref, rpb_ref, cmask_ref, out_ref, kb_ref, vb_ref, tab_ref, bias_ref, *, scale):
    seq = q_ref.shape[0]
    n_groups = seq // GRID_W // NA_GROUP
    group = NA_GROUP * GRID_W
    slab = NA_SLAB_ROWS * GRID_W
    kb_ref[...] = k_ref[...].astype(BF16)
    vb_ref[...] = v_ref[...].astype(BF16)
    _na_build_bias(rpb_ref, cmask_ref, tab_ref, bias_ref)

    def group_step(gi):
        slab_start = jnp.clip(gi - 1, 0, n_groups - NA_SLAB_ROWS // NA_GROUP) * group
        variant = jnp.where(gi == 0, 0, jnp.where(gi == n_groups - 1, 2, 1))
        qrows = pl.ds(pl.multiple_of(gi * group, group), group)
        krows = pl.ds(pl.multiple_of(slab_start, group), slab)
        q = (q_ref[qrows, :] * scale).astype(BF16)
        s = _dot_nt(q, kb_ref[krows, :]) + bias_ref[variant]
        yield
        m = jnp.max(s, axis=-1, keepdims=True)
        p = jnp.exp2(s - m)
        yield
        denom = jnp.sum(p, axis=-1, keepdims=True)
        o = _dot(p.astype(BF16), vb_ref[krows, :]) / denom
        yield
        out_ref[qrows, :] = o.astype(out_ref.dtype)

    def body(pi, carry):
        _run_interleaved([group_step(pi * NA_INTERLEAVE + c) for c in range(NA_INTERLEAVE)])
        return carry

    assert n_groups % NA_INTERLEAVE == 0
    lax.fori_loop(0, n_groups // NA_INTERLEAVE, body, 0, unroll=max(1, NA_GROUP_UNROLL // NA_INTERLEAVE))


def neighbourhood_attention(proj, rpb, n_heads, col0, t):
    b = proj.shape[1] // t
    n_rows = t // GRID_W
    assert NA_GROUP == NA_WIN_ROWS // 2 and NA_SLAB_ROWS % NA_GROUP == 0 and (NA_SLAB_ROWS * GRID_W) % 128 == 0
    assert t % GRID_W == 0 and n_rows % NA_GROUP == 0 and n_rows >= NA_SLAB_ROWS + NA_GROUP
    n_dr, n_dc = rpb.shape[1:]
    assert (n_dr, n_dc) == (2 * NA_WIN_ROWS - 1, 2 * NA_WIN_COLS - 1)
    rpb_pad = jnp.pad(rpb.astype(F32), ((0, 0), (0, 1), (0, 128 - n_dc)))
    cmask = _na_col_mask()

    def col(group):
        return pl.BlockSpec((None, t, HEAD_DIM), lambda bi, hi, g=group: (col0 + g * n_heads + hi, bi, 0))

    return pl.pallas_call(
        functools.partial(_na_kernel, scale=HEAD_DIM ** -0.5 * LOG2E),
        grid=(b, n_heads),
        in_specs=[col(0), col(1), col(2),
                  pl.BlockSpec((None, n_dr + 1, 128), lambda bi, hi: (hi, 0, 0)),
                  pl.BlockSpec(cmask.shape, lambda bi, hi: (0, 0))],
        out_specs=pl.BlockSpec((None, t, HEAD_DIM), lambda bi, hi: (bi, 0, hi)),
        out_shape=jax.ShapeDtypeStruct((b, t, n_heads * HEAD_DIM), BF16),
        scratch_shapes=[pltpu.VMEM((t, HEAD_DIM), BF16), pltpu.VMEM((t, HEAD_DIM), BF16),
                        pltpu.VMEM((n_dr + 1, GRID_W, 128), F32),
                        pltpu.VMEM((3, NA_GROUP * GRID_W, NA_SLAB_ROWS * GRID_W), F32)],
        compiler_params=_params("parallel", "parallel"),
        name="neighbourhood_attention",
    )(proj, proj, proj, rpb_pad, cmask)


def _swiglu_ffn(u, w_gate, w_up, w_down):
    act, w_down16 = gateup(u, w_gate, w_up, w_down)
    return matmul(act, w_down16, BF16, 512, 1024, "ffn_down", rhs_resident=True)


def kernel(x, ffn1_norm_pre, ffn1_w_gate, ffn1_w_up, ffn1_w_down, ffn1_norm_post, mix_norm_pre, w_in, hgrn_lb_logits, hgrn_head_norm, na_rpb, w_out, mix_norm_post, ffn2_norm_pre, ffn2_w_gate, ffn2_w_up, ffn2_w_down, ffn2_norm_post):
    b, t, d = x.shape
    depth = w_in.shape[0]
    hg_width = hgrn_lb_logits.shape[-1]
    hg_heads = hg_width // HEAD_DIM
    na_heads = na_rpb.shape[1]
    h = x.reshape(b * t, d)
    u = rmsnorm_cast(h, ffn1_norm_pre[0])
    for layer in range(depth):
        ff = _swiglu_ffn(u, ffn1_w_gate[layer], ffn1_w_up[layer], ffn1_w_down[layer])
        h, u = resnorm(h, ff, ffn1_norm_post[layer], FFN_RES_WEIGHT, mix_norm_pre[layer])

        proj = matmul(u, w_in[layer], F32, 2048, 512, "in_proj", head_major=True)
        lb = jnp.cumsum(jax.nn.softmax(hgrn_lb_logits.astype(F32), axis=0), axis=0)[layer]
        o_hg = hgrn2(proj, lb, hgrn_head_norm[layer], hg_heads, t)
        o_na = neighbourhood_attention(proj, na_rpb[layer], na_heads, 5 * hg_heads, t)
        mixed = matmul_concat(o_hg.reshape(b * t, -1), o_na.reshape(b * t, -1), w_out[layer],
                              BF16, 1024, 512, "out_proj")
        h, u = resnorm(h, mixed, mix_norm_post[layer], 1.0, ffn2_norm_pre[layer])

        ff = _swiglu_ffn(u, ffn2_w_gate[layer], ffn2_w_up[layer], ffn2_w_down[layer])
        if layer + 1 < depth:
            h, u = resnorm(h, ff, ffn2_norm_post[layer], FFN_RES_WEIGHT, ffn1_norm_pre[layer + 1])
        else:
            h = resnorm(h, ff, ffn2_norm_post[layer], FFN_RES_WEIGHT)
    return h.reshape(b, t, d)
```
